```python
import math
import jax, jax.numpy as jnp
from jax import lax
import numpy as np

D_MODEL = 2048
BATCH = 4
SEQ = 2048
DEPTH = 2
DEC_BATCH = 32
DEC_SEQ = 8
PAST_LEN = 8192
PAGE_SIZE = 128

N_META = 16
WB = D_MODEL // 4
WA = (D_MODEL - WB) // 2
WC = D_MODEL - WA - WB
NA = 64
HA = WA // NA
LORA_W = 64
LORA_A = 64
LORA_G = 128
DVB = 128
DKB = DVB // 2
HB = WB // DVB
Q_BLOCK = 128
MASK_VALUE = -1e30
DVC = 128
DKC = 128
HC = WC // DVC
HG_CHUNK = 16
F_FLOOR = 1e-30
D_FF = ((8 * D_MODEL // 3 + 255) // 256) * 256
CONV_W = 3
PA = 3 * WA + LORA_W + LORA_A + LORA_G
PB = 2 * HB * 2 * DKB + HB * DVB
PC = 2 * HC * DKC + 2 * WC
P_TOTAL = PA + PB + PC

kernel_name = 'hybrid_rwkv7_diffattn_hgrn2_step'


def rms_norm(x, w, eps=1e-6):
    xf = x.astype(jnp.float32)
    y = xf * lax.rsqrt(jnp.mean(xf * xf, axis=-1, keepdims=True) + eps)
    return (y * w.astype(jnp.float32)).astype(x.dtype)


def rwkv7_mix(pa, prev_row, s0, mu, w0, w2, a0, a2, g2, k_k, k_a, r_k, ln_w, ln_b):
    b, t, _ = pa.shape
    prev = jnp.concatenate([prev_row[:, None].astype(pa.dtype), pa[:, :-1]], axis=1)
    xm = pa + (prev - pa) * mu
    r, k, v, wd, ad, gd = jnp.split(xm, [WA, 2 * WA, 3 * WA, 3 * WA + LORA_W, 3 * WA + LORA_W + LORA_A], axis=-1)
    w = -jax.nn.softplus(-(w0 + jnp.tanh(wd) @ w2)) - 0.5
    decay = jnp.exp(-jnp.exp(w.astype(jnp.float32)))
    a = jax.nn.sigmoid(a0 + ad @ a2)
    g = jax.nn.sigmoid(gd) @ g2
    heads = lambda z: z.astype(jnp.float32).reshape(b, t, HA, NA)
    kk = heads(k * k_k)
    kk = kk / jnp.maximum(jnp.sqrt(jnp.sum(kk * kk, axis=-1, keepdims=True)), 1e-12)
    k = k * (1 + (a - 1) * k_a)
    rh, kh, vh, ah, dh = heads(r), heads(k), heads(v), heads(a), heads(decay)

    def step(s, inp):
        r_t, w_t, k_t, v_t, kk_t, b_t = inp
        sa = jnp.einsum('bhij,bhj->bhi', s, -kk_t)
        s = s * w_t[:, :, None, :] + sa[..., None] * b_t[:, :, None, :] + v_t[..., None] * k_t[:, :, None, :]
        return s, jnp.einsum('bhij,bhj->bhi', s, r_t)

    xs = tuple(jnp.moveaxis(z, 1, 0) for z in (rh, dh, kh, vh, kk, kk * ah))
    s, o = lax.scan(step, s0.astype(jnp.float32), xs)
    o = jnp.moveaxis(o, 0, 1)
    mean = jnp.mean(o, axis=-1, keepdims=True)
    var = jnp.mean(jnp.square(o - mean), axis=-1, keepdims=True)
    o = ((o - mean) * lax.rsqrt(var + 64e-5)).reshape(b, t, WA) * ln_w + ln_b
    o = o + (jnp.sum(rh * kh * r_k, axis=-1, keepdims=True) * vh).reshape(b, t, WA)
    return (o * g).astype(pa.dtype), s.astype(s0.dtype), pa[:, -1]


def diff_attention(q, k, v, lam, q_offset):
    b, lq = q.shape[:2]
    lk = k.shape[1]
    qb = min(Q_BLOCK, lq)
    nb = -(-lq // qb)
    qp = jnp.pad(q, ((0, 0), (0, nb * qb - lq), (0, 0), (0, 0), (0, 0)))
    qp = jnp.moveaxis(qp.reshape(b, nb, qb, HB, 2, DKB), 1, 0)
    starts = q_offset + qb * jnp.arange(nb)
    kpos = jnp.arange(lk)
    kf = k.astype(jnp.float32)
    vf = v.astype(jnp.float32)

    def block(args):
        qblk, start = args
        s = jnp.einsum('bqhmd,bkhmd->bhmqk', qblk.astype(jnp.float32), kf) * (DKB ** -0.5)
        qpos = start + jnp.arange(qb)
        s = jnp.where(kpos[None, :] <= qpos[:, None], s, MASK_VALUE)
        p = jax.nn.softmax(s, axis=-1)
        att = p[:, :, 0] - lam * p[:, :, 1]
        return jnp.einsum('bhqk,bkhd->bqhd', att, vf)

    o = lax.map(block, (qp, starts))
    return jnp.moveaxis(o, 0, 1).reshape(b, nb * qb, HB, DVB)[:, :lq].astype(q.dtype)


def hgrn2_mix(pc, s0, lb, norm_w, chunk):
    b, t, _ = pc.shape
    q, fl, i, g = jnp.split(pc, [HC * DKC, 2 * HC * DKC, 2 * HC * DKC + WC], axis=-1)
    fl = fl.astype(jnp.float32)
    f = lb + (1.0 - lb) * jax.nn.sigmoid(fl)
    log_f = jnp.log(jnp.maximum(f, F_FLOOR))
    key = (1.0 - lb) * jax.nn.sigmoid(-fl)
    nc = t // chunk
    to_chunks = lambda z: jnp.moveaxis(z.astype(jnp.float32).reshape(b, nc, chunk, HC, -1), 1, 0)
    xs = (to_chunks(q), to_chunks(log_f), to_chunks(key), to_chunks(i))
    causal = jnp.tril(jnp.ones((chunk, chunk), dtype=bool))
    mid = (chunk - 1) // 2

    def step(s, inp):
        qc, lfc, kc, ic = inp
        cum = jnp.cumsum(lfc, axis=1)
        anchor = cum[:, mid:mid + 1]
        att = jnp.einsum('bthd,bshd->bhts', qc * jnp.exp(cum - anchor), kc * jnp.exp(anchor - cum))
        att = jnp.where(causal, att, 0.0)
        o = jnp.einsum('bhts,bshv->bthv', att, ic) + jnp.einsum('bthd,bhdv->bthv', qc * jnp.exp(cum), s)
        last = cum[:, -1]
        s = jnp.exp(last)[..., None] * s + jnp.einsum('bshd,bshv->bhdv', kc * jnp.exp(last[:, None] - cum), ic)
        return s, o

    s, o = lax.scan(step, s0.astype(jnp.float32), xs)
    o = rms_norm(jnp.moveaxis(o, 0, 1).reshape(b, t, HC, DVC), norm_w)
    o = o.reshape(b, t, WC) * jax.nn.silu(g.astype(jnp.float32))
    return o.astype(pc.dtype), s.astype(s0.dtype)


def conv_ffn(xn, buf, w_up, w_conv, b_conv, w_down):
    t = xn.shape[1]
    u = xn @ w_up
    ue = jnp.concatenate([buf.astype(u.dtype), u], axis=1)
    c = b_conv
    for j in range(CONV_W):
        c = c + ue[:, j:j + t] * w_conv[j]
    gate, val = jnp.split(c, 2, axis=-1)
    return (jax.nn.silu(gate) * val) @ w_down, ue[:, t:]


def run_trunk(h, past_kv, rwkv0, shift0, hgrn0, conv0, q_offset, W):
    b, t, _ = h.shape
    chunk = math.gcd(HG_CHUNK, t)
    lbs = jax.nn.softmax(W['hg_lb'].astype(jnp.float32), axis=0)
    lbs = jnp.cumsum(lbs, axis=0) - lbs[0]
    ks, vs, sas, shs, scs, cvs = [], [], [], [], [], []
    for l in range(DEPTH):
        xn = rms_norm(h, W['norm1'][l])
        p = xn @ W['w_in'][l]
        pa, pb, pc = jnp.split(p, [PA, PA + PB], axis=-1)
        ya, sa, sh = rwkv7_mix(pa, shift0[l], rwkv0[l], W['rw_mu'][l], W['rw_w0'][l], W['rw_w2'][l],
                               W['rw_a0'][l], W['rw_a2'][l], W['rw_g2'][l], W['rw_kk'][l], W['rw_ka'][l],
                               W['rw_rk'][l], W['rw_lnw'][l], W['rw_lnb'][l])
        qd, kd, vd = jnp.split(pb, [2 * HB * DKB, 4 * HB * DKB], axis=-1)
        q = qd.reshape(b, t, HB, 2, DKB)
        k = kd.reshape(b, t, HB, 2 * DKB)
        v = vd.reshape(b, t, HB, DVB)
        if past_kv is None:
            k_all, v_all = k, v
        else:
            pk, pv = past_kv(l)
            k_all = jnp.concatenate([pk.astype(k.dtype), k], axis=1)
            v_all = jnp.concatenate([pv.astype(v.dtype), v], axis=1)
        lam_init = 0.8 - 0.6 * math.exp(-0.3 * l)
        lam = (jnp.exp(jnp.sum(W['da_lq1'][l].astype(jnp.float32) * W['da_lk1'][l].astype(jnp.float32)))
               - jnp.exp(jnp.sum(W['da_lq2'][l].astype(jnp.float32) * W['da_lk2'][l].astype(jnp.float32))) + lam_init)
        yb = diff_attention(q, k_all.reshape(b, -1, HB, 2, DKB), v_all, lam, q_offset)
        yb = (rms_norm(yb, W['da_subln'][l], 1e-5) * (1.0 - lam_init)).reshape(b, t, WB)
        yc, sc = hgrn2_mix(pc, hgrn0[l], lbs[l], W['hg_norm'][l], chunk)
        h = h + jnp.concatenate([ya, yb, yc], axis=-1) @ W['w_out'][l]
        f, cb = conv_ffn(rms_norm(h, W['norm2'][l]), conv0[l], W['ffn_up'][l], W['ffn_conv'][l],
                         W['ffn_conv_b'][l], W['ffn_down'][l])
        h = h + f
        ks.append(k)
        vs.append(v)
        sas.append(sa)
        shs.append(sh)
        scs.append(sc)
        cvs.append(cb)
    return (rms_norm(h, W['final_norm']), jnp.stack(ks), jnp.stack(vs), jnp.stack(sas),
            jnp.stack(shs), jnp.stack(scs), jnp.stack(cvs))


def setup_inputs(seed: int = 0) -> dict:
    key = jax.random.key(seed)
    ks = jax.random.split(key, 40)
    nrm = lambda i, shape, scale=1.0: jax.random.normal(ks[i], shape, jnp.float32) * scale
    n_pages = PAST_LEN // PAGE_SIZE
    n_used = DEC_BATCH * n_pages
    n_pool = n_used + max(1, n_used // 4)
    page_table = jax.random.permutation(ks[0], n_pool)[:n_used].reshape(DEC_BATCH, n_pages).astype(jnp.int32)
    return {
        'x_prompt': nrm(1, (BATCH, SEQ, D_MODEL)),
        'x_sample': nrm(2, (DEC_BATCH, DEC_SEQ, D_MODEL)),
        'cache_k': nrm(3, (DEPTH, n_pool, PAGE_SIZE, HB, 2 * DKB)),
        'cache_v': nrm(4, (DEPTH, n_pool, PAGE_SIZE, HB, DVB)),
        'state_rwkv': nrm(5, (DEPTH, DEC_BATCH, HA, NA, NA), 0.5),
        'state_shift': nrm(6, (DEPTH, DEC_BATCH, PA)),
        'state_hgrn': nrm(7, (DEPTH, DEC_BATCH, HC, DKC, DVC), 0.5),
        'state_conv': nrm(8, (DEPTH, DEC_BATCH, CONV_W - 1, 2 * D_FF)),
        'page_table': page_table,
        'meta': nrm(9, (N_META, D_MODEL)),
        'norm1': 1.0 + nrm(10, (DEPTH, D_MODEL), 0.02),
        'w_in': nrm(11, (DEPTH, D_MODEL, P_TOTAL), D_MODEL ** -0.5),
        'rw_mu': jax.random.uniform(ks[12], (DEPTH, PA), jnp.float32),
        'rw_w0': jax.random.uniform(ks[13], (DEPTH, WA), jnp.float32, -6.0, 0.0),
        'rw_w2': nrm(14, (DEPTH, LORA_W, WA), 0.5 * LORA_W ** -0.5),
        'rw_a0': nrm(15, (DEPTH, WA), 0.5),
        'rw_a2': nrm(16, (DEPTH, LORA_A, WA), 0.5 * LORA_A ** -0.5),
        'rw_g2': nrm(17, (DEPTH, LORA_G, WA), LORA_G ** -0.5),
        'rw_kk': 0.85 + nrm(18, (DEPTH, WA), 0.05),
        'rw_ka': 1.0 + nrm(19, (DEPTH, WA), 0.05),
        'rw_rk': nrm(20, (DEPTH, HA, NA), 0.1),
        'rw_lnw': 1.0 + nrm(21, (DEPTH, WA), 0.02),
        'rw_lnb': nrm(22, (DEPTH, WA), 0.01),
        'da_lq1': nrm(23, (DEPTH, DKB), 0.1),
        'da_lk1': nrm(24, (DEPTH, DKB), 0.1),
        'da_lq2': nrm(25, (DEPTH, DKB), 0.1),
        'da_lk2': nrm(26, (DEPTH, DKB), 0.1),
        'da_subln': 1.0 + nrm(27, (DEPTH, DVB), 0.02),
        'hg_lb': nrm(28, (DEPTH, HC * DKC)),
        'hg_norm': 1.0 + nrm(29, (DEPTH, DVC), 0.02),
        'w_out': nrm(30, (DEPTH, D_MODEL, D_MODEL), D_MODEL ** -0.5),
        'norm2': 1.0 + nrm(31, (DEPTH, D_MODEL), 0.02),
        'ffn_up': nrm(32, (DEPTH, D_MODEL, 2 * D_FF), D_MODEL ** -0.5),
        'ffn_conv': nrm(33, (DEPTH, CONV_W, 2 * D_FF), CONV_W ** -0.5),
        'ffn_conv_b': nrm(34, (DEPTH, 2 * D_FF), 0.01),
        'ffn_down': nrm(35, (DEPTH, D_FF, D_MODEL), D_FF ** -0.5),
        'final_norm': 1.0 + nrm(36, (D_MODEL,), 0.02),
    }


def reference(x_prompt, x_sample, cache_k, cache_v, state_rwkv, state_shift, state_hgrn, state_conv, page_table,
              meta, norm1, w_in, rw_mu, rw_w0, rw_w2, rw_a0, rw_a2, rw_g2, rw_kk, rw_ka, rw_rk, rw_lnw, rw_lnb,
              da_lq1, da_lk1, da_lq2, da_lk2, da_subln, hg_lb, hg_norm, w_out, norm2, ffn_up, ffn_conv,
              ffn_conv_b, ffn_down, final_norm):
    W = {'norm1': norm1, 'w_in': w_in, 'rw_mu': rw_mu, 'rw_w0': rw_w0, 'rw_w2': rw_w2, 'rw_a0': rw_a0,
         'rw_a2': rw_a2, 'rw_g2': rw_g2, 'rw_kk': rw_kk, 'rw_ka': rw_ka, 'rw_rk': rw_rk, 'rw_lnw': rw_lnw,
         'rw_lnb': rw_lnb, 'da_lq1': da_lq1, 'da_lk1': da_lk1, 'da_lq2': da_lq2, 'da_lk2': da_lk2,
         'da_subln': da_subln, 'hg_lb': hg_lb, 'hg_norm': hg_norm, 'w_out': w_out, 'norm2': norm2,
         'ffn_up': ffn_up, 'ffn_conv': ffn_conv, 'ffn_conv_b': ffn_conv_b, 'ffn_down': ffn_down,
         'final_norm': final_norm}
    bp = x_prompt.shape[0]
    dt = x_prompt.dtype
    h = jnp.concatenate([jnp.broadcast_to(meta[None].astype(dt), (bp, N_META, D_MODEL)), x_prompt], axis=1)
    yp, p_k, p_v, p_rwkv, p_shift, p_hgrn, p_conv = run_trunk(
        h, None,
        jnp.zeros((DEPTH, bp, HA, NA, NA), dt), jnp.zeros((DEPTH, bp, PA), dt),
        jnp.zeros((DEPTH, bp, HC, DKC, DVC), dt), jnp.zeros((DEPTH, bp, CONV_W - 1, 2 * D_FF), dt),
        0, W)
    y_prompt = yp[:, N_META:]
    bd = page_table.shape[0]
    past_len = page_table.shape[1] * PAGE_SIZE

    def past_kv(l):
        pk = cache_k[l][page_table].reshape(bd, past_len, HB, 2 * DKB)
        pv = cache_v[l][page_table].reshape(bd, past_len, HB, DVB)
        return pk, pv

    y_sample, s_k, s_v, s_rwkv, s_shift, s_hgrn, s_conv = run_trunk(
        x_sample, past_kv, state_rwkv, state_shift, state_hgrn, state_conv, past_len, W)
    return (y_prompt, y_sample, p_k, p_v, p_rwkv, p_shift, p_hgrn, p_conv, s_k, s_v, s_rwkv, s_shift, s_hgrn, s_conv)
```

```python
import functools
import math

import jax
import jax.numpy as jnp
from jax import lax
from jax.experimental import pallas as pl
from jax.experimental.pallas import tpu as pltpu

F32 = jnp.float32
BF16 = jnp.bfloat16
HIGHEST = lax.Precision.HIGHEST

D_MODEL = 2048
DEPTH = 2
N_META = 16
PAGE_SIZE = 128
WB = D_MODEL // 4
WA = (D_MODEL - WB) // 2
WC = D_MODEL - WA - WB
NA = 64
HA = WA // NA
LORA_W = 64
LORA_A = 64
LORA_G = 128
DVB = 128
DKB = DVB // 2
HB = WB // DVB
MASK_VALUE = -1e30
DVC = 128
DKC = 128
HC = WC // DVC
F_FLOOR = 1e-30
D_FF = ((8 * D_MODEL // 3 + 255) // 256) * 256
CONV_W = 3
PA = 3 * WA + LORA_W + LORA_A + LORA_G
PB = 2 * HB * 2 * DKB + HB * DVB
PC = 2 * HC * DKC + 2 * WC

LANES = 128
HEAD_PAIRS = HA // 2
VMEM_LIMIT = 56 * 1024 * 1024


def _dot(a, b, precision=None):
    return jnp.dot(a, b, preferred_element_type=F32, precision=precision)


def _dot_nt(a, b, precision=None):
    return lax.dot_general(a, b, (((1,), (1,)), ((), ())), preferred_element_type=F32, precision=precision)


def _dot_tn(a, b, precision=None):
    return lax.dot_general(a, b, (((0,), (0,)), ((), ())), preferred_element_type=F32, precision=precision)


def _sigmoid(x):
    return 1.0 / (1.0 + jnp.exp(-x))


def _softplus(x):
    return jnp.maximum(x, 0.0) + jnp.log(1.0 + jnp.exp(-jnp.abs(x)))


def _mm_body(*refs, norm, eps, residual, sub):
    refs = list(refs)
    x_ref = refs.pop(0)
    g_ref = refs.pop(0) if norm else None
    w_ref = refs.pop(0)
    r_ref = refs.pop(0) if residual else None
    o_ref, xs_ref = refs

    @pl.when(pl.program_id(1) == 0)
    def _():
        tm = x_ref.shape[0]
        for s in range(0, tm, sub):
            x = x_ref[s:s + sub, :].astype(F32)
            if norm:
                ms = jnp.mean(x * x, axis=-1, keepdims=True)
                x = x * lax.rsqrt(ms + eps) * g_ref[...]
            xs_ref[s:s + sub, :] = x.astype(BF16)

    acc = _dot(xs_ref[...], w_ref[...].astype(BF16))
    if residual:
        acc = acc + r_ref[...]
    o_ref[...] = acc


def _row_tile(m):
    for t in (1032, 688, 512, 344, 256, 128, 64, 32, 16, 8):
        if m % t == 0:
            return t
    raise ValueError(m)


def _matmul(x, w, layer, *, n_cols, col_off=0, tn=512, gain=None, eps=1e-6, residual=None):
    m, k = x.shape
    tm = _row_tile(m)
    sub = 344 if tm % 344 == 0 else tm
    norm = gain is not None
    in_specs = [pl.BlockSpec((tm, k), lambda i, j: (i, 0))]
    args = [x]
    if norm:
        in_specs.append(pl.BlockSpec((None, 1, k), lambda i, j: (layer, 0, 0)))
        args.append(gain.reshape(gain.shape[0], 1, k))
    in_specs.append(pl.BlockSpec((None, k, tn), lambda i, j: (layer, 0, j + col_off)))
    args.append(w)
    if residual is not None:
        in_specs.append(pl.BlockSpec((tm, tn), lambda i, j: (i, j)))
        args.append(residual)
    return pl.pallas_call(
        functools.partial(_mm_body, norm=norm, eps=eps, residual=residual is not None, sub=sub),
        grid=(m // tm, n_cols // tn),
        in_specs=in_specs,
        out_specs=pl.BlockSpec((tm, tn), lambda i, j: (i, j)),
        out_shape=jax.ShapeDtypeStruct((m, n_cols), F32),
        scratch_shapes=[pltpu.VMEM((tm, k), BF16)],
        compiler_params=pltpu.CompilerParams(
            dimension_semantics=("parallel", "arbitrary"), vmem_limit_bytes=VMEM_LIMIT),
        name="matmul",
    )(*args)


def _norm_body(x_ref, g_ref, o_ref, *, eps):
    x = x_ref[...]
    ms = jnp.mean(x * x, axis=-1, keepdims=True)
    o_ref[...] = x * lax.rsqrt(ms + eps) * g_ref[...]


def _rmsnorm(x, gain, eps=1e-6):
    m, k = x.shape
    tm = 344 if m % 344 == 0 else _row_tile(m)
    return pl.pallas_call(
        functools.partial(_norm_body, eps=eps),
        grid=(m // tm,),
        in_specs=[pl.BlockSpec((tm, k), lambda i: (i, 0)), pl.BlockSpec((1, k), lambda i: (0, 0))],
        out_specs=pl.BlockSpec((tm, k), lambda i: (i, 0)),
        out_shape=jax.ShapeDtypeStruct((m, k), F32),
        compiler_params=pltpu.CompilerParams(dimension_semantics=("parallel",)),
        name="final_norm",
    )(x, gain.reshape(1, k))


def _conv_body(ug_ref, uv_ref, bg_ref, bv_ref, wg_ref, wv_ref, cg_ref, cv_ref, o_ref):
    def conv(u_ref, buf_ref, w_ref, c_ref):
        u = u_ref[...]
        buf = buf_ref[...]
        w = w_ref[...]
        row = lax.broadcasted_iota(jnp.int32, (1, u.shape[1], 1), 1)
        b0 = buf[:, 0:1, :]
        b1 = buf[:, 1:2, :]
        u1 = jnp.where(row == 0, b1, pltpu.roll(u, 1, 1))
        u2 = jnp.where(row == 0, b0, jnp.where(row == 1, b1, pltpu.roll(u, 2, 1)))
        return c_ref[...] + u2 * w[0:1, :] + u1 * w[1:2, :] + u * w[2:3, :]

    gate = conv(ug_ref, bg_ref, wg_ref, cg_ref)
    val = conv(uv_ref, bv_ref, wv_ref, cv_ref)
    o_ref[...] = (gate * _sigmoid(gate) * val).astype(o_ref.dtype)


def _conv_gate(u3, buf, w_conv, b_conv, layer, *, tn=512):
    b, t, _ = u3.shape
    bb = b if t <= 64 else 1
    nj = D_FF // tn
    b_conv3 = b_conv.reshape(DEPTH, 1, 2 * D_FF)
    blk = lambda off: pl.BlockSpec((bb, t, tn), lambda i, j: (i, 0, j + off))
    bufs = lambda off: pl.BlockSpec((None, bb, CONV_W - 1, tn), lambda i, j: (layer, i, 0, j + off))
    ws = lambda off: pl.BlockSpec((None, CONV_W, tn), lambda i, j: (layer, 0, j + off))
    cs = lambda off: pl.BlockSpec((None, 1, tn), lambda i, j: (layer, 0, j + off))
    return pl.pallas_call(
        _conv_body,
        grid=(b // bb, nj),
        in_specs=[blk(0), blk(nj), bufs(0), bufs(nj), ws(0), ws(nj), cs(0), cs(nj)],
        out_specs=pl.BlockSpec((bb, t, tn), lambda i, j: (i, 0, j)),
        out_shape=jax.ShapeDtypeStruct((b, t, D_FF), BF16),
        compiler_params=pltpu.CompilerParams(
            dimension_semantics=("parallel", "parallel"), vmem_limit_bytes=VMEM_LIMIT),
        name="conv_gate",
    )(u3, u3, buf, buf, w_conv, w_conv, b_conv3, b_conv3)


def _seg_sum64(x, bd):
    return jnp.concatenate(
        [_dot(x[:, j * LANES:(j + 1) * LANES], bd, HIGHEST) for j in range(x.shape[1] // LANES)], axis=1)


def _rwkv_body(x_ref, shift_ref, s0_ref, mu_ref, w0_ref, wa2_ref, a0_ref, g2_ref, kk_ref, ka_ref, rk_ref,
               lnw_ref, lnb_ref, y_ref, sout_ref, prev_scr, s_scr, *, chunk, n_sq):
    c = pl.program_id(1)

    @pl.when(c == 0)
    def _():
        prev_scr[...] = shift_ref[0]
        s_scr[...] = s0_ref[0]

    x = x_ref[0]
    row = lax.broadcasted_iota(jnp.int32, (chunk, 1), 0)
    prev = jnp.where(row == 0, prev_scr[...], pltpu.roll(x, 1, 0))
    prev_scr[...] = x[chunk - 1:chunk, :]
    xm = x + (prev - x) * mu_ref[...]
    r = xm[:, :WA]
    k = xm[:, WA:2 * WA]
    v = xm[:, 2 * WA:3 * WA]
    wa_in = xm[:, 3 * WA:3 * WA + LANES]
    gd = xm[:, 3 * WA + LANES:]

    lane = lax.broadcasted_iota(jnp.int32, (1, LANES), 1)
    lo = lane < NA
    wa2 = wa2_ref[...]
    z = w0_ref[...] + _dot(jnp.where(lo, jnp.tanh(wa_in), 0.0), wa2, HIGHEST)
    w = -_softplus(-z) - 0.5
    logd = -jnp.exp(w)
    a = _sigmoid(a0_ref[...] + _dot(jnp.where(lo, 0.0, wa_in), wa2, HIGHEST))
    g = _dot(_sigmoid(gd), g2_ref[...], HIGHEST)

    ri = lax.broadcasted_iota(jnp.int32, (LANES, LANES), 0)
    ci = lax.broadcasted_iota(jnp.int32, (LANES, LANES), 1)
    bd_mask = (ri < NA) == (ci < NA)
    bd = jnp.where(bd_mask, 1.0, 0.0).astype(F32)
    eye = ri == ci

    kkr = k * kk_ref[...]
    kk = kkr / jnp.maximum(jnp.sqrt(_seg_sum64(kkr * kkr, bd)), 1e-12)
    k2 = k * (1.0 + (a - 1.0) * ka_ref[...])
    b = kk * a

    ti = lax.broadcasted_iota(jnp.int32, (chunk, chunk), 0)
    si = lax.broadcasted_iota(jnp.int32, (chunk, chunk), 1)
    incl = si <= ti
    strict = si < ti
    tri = jnp.where(incl, 1.0, 0.0).astype(F32)
    eye_c = jnp.where(ti == si, 1.0, 0.0).astype(F32)

    cl = _dot(tri, logd, HIGHEST)
    cl_end = cl[chunk - 1:chunk, :]
    e_neg = jnp.exp(-cl)
    e_end = jnp.exp(cl_end - cl)
    alpha = kk * jnp.exp(cl - logd)
    rho = r * jnp.exp(cl)
    beta = b * e_neg
    kappa = k2 * e_neg
    beta_e = b * e_end
    kappa_e = k2 * e_end
    p_end = jnp.exp(cl_end)

    outs = []
    for j in range(HEAD_PAIRS):
        sl = slice(j * LANES, (j + 1) * LANES)
        a2_, r2_, b2_, k2_, v2_ = alpha[:, sl], rho[:, sl], beta[:, sl], kappa[:, sl], v[:, sl]
        be2, ke2 = beta_e[:, sl], kappa_e[:, sl]
        s_pair = s_scr[j]
        ar = jnp.concatenate([a2_, r2_], axis=0)
        per_head = []
        for first in (True, False):
            m = lo if first else jnp.logical_not(lo)
            arm = jnp.where(m, ar, 0.0)
            mb = _dot_nt(arm, b2_, HIGHEST)
            mk = _dot_nt(arm, k2_, HIGHEST)
            mab = jnp.where(strict, mb[:chunk], 0.0)
            mrb = jnp.where(incl, mb[chunk:], 0.0)
            mak = jnp.where(strict, mk[:chunk], 0.0)
            mrk = jnp.where(incl, mk[chunk:], 0.0)
            pw = -mab
            tinv = eye_c + pw
            for _ in range(n_sq):
                pw = _dot(pw, pw, HIGHEST)
                tinv = tinv + _dot(tinv, pw, HIGHEST)
            xh = _dot(mak, v2_, HIGHEST)
            wu = _dot(tinv, jnp.concatenate([a2_, xh], axis=1), HIGHEST)
            y2 = _dot(mrb, wu, HIGHEST)
            rw = r2_ - y2[:, :LANES]
            o0 = _dot(mrk, v2_, HIGHEST) - y2[:, LANES:]
            per_head.append((wu[:, :LANES], wu[:, LANES:], rw, o0))
        w2_ = jnp.where(lo, per_head[0][0], per_head[1][0])
        u02 = jnp.where(lo, per_head[0][1], per_head[1][1])
        rw2 = jnp.where(lo, per_head[0][2], per_head[1][2])
        o02 = jnp.where(lo, per_head[0][3], per_head[1][3])
        outs.append(_dot_nt(rw2, s_pair, HIGHEST) + o02)
        wtb = _dot_tn(w2_, be2, HIGHEST)
        phi = jnp.where(bd_mask, jnp.where(eye, p_end[:, sl], 0.0) - wtb, 0.0)
        gm = jnp.where(bd_mask, _dot_tn(v2_, ke2, HIGHEST) - _dot_tn(u02, be2, HIGHEST), 0.0)
        s_scr[j] = _dot(s_pair, phi, HIGHEST) + gm
    o = jnp.concatenate(outs, axis=1)

    mean = _seg_sum64(o, bd) * (1.0 / NA)
    oc = o - mean
    var = _seg_sum64(oc * oc, bd) * (1.0 / NA)
    o = oc * lax.rsqrt(var + 64e-5) * lnw_ref[...] + lnb_ref[...]
    o = o + _seg_sum64(r * k2 * rk_ref[...], bd) * v
    y_ref[0] = (o * g).astype(y_ref.dtype)

    @pl.when(c == pl.num_programs(1) - 1)
    def _():
        sout_ref[0] = s_scr[...]


def _n_squarings(chunk):
    n, reach = 0, 1
    while reach < chunk - 1:
        n += 1
        reach = 2 * reach + 1
    return n


def _pair_states(s):
    b = s.shape[0]
    s = s.reshape(b, HEAD_PAIRS, 2, NA, NA)
    z = jnp.zeros((b, HEAD_PAIRS, NA, NA), s.dtype)
    top = jnp.concatenate([s[:, :, 0], z], axis=-1)
    bot = jnp.concatenate([z, s[:, :, 1]], axis=-1)
    return jnp.concatenate([top, bot], axis=-2)


def _unpair_states(sp):
    b = sp.shape[0]
    s = jnp.stack([sp[:, :, :NA, :NA], sp[:, :, NA:, NA:]], axis=2)
    return s.reshape(b, HA, NA, NA)


def _rwkv(pa3, shift0, s0, params, layer, chunk):
    b, t, _ = pa3.shape
    nc = t // chunk
    vec = lambda n: pl.BlockSpec((None, 1, n), lambda i, c: (layer, 0, 0))
    mat = lambda r, n: pl.BlockSpec((None, r, n), lambda i, c: (layer, 0, 0))
    y, s_out = pl.pallas_call(
        functools.partial(_rwkv_body, chunk=chunk, n_sq=_n_squarings(chunk)),
        grid=(b, nc),
        in_specs=[
            pl.BlockSpec((1, chunk, PA), lambda i, c: (i, c, 0)),
            pl.BlockSpec((1, 1, PA), lambda i, c: (i, 0, 0)),
            pl.BlockSpec((1, HEAD_PAIRS, LANES, LANES), lambda i, c: (i, 0, 0, 0)),
            vec(PA), vec(WA), mat(LANES, WA), vec(WA), mat(LORA_G, WA), vec(WA), vec(WA), vec(WA), vec(WA), vec(WA),
        ],
        out_specs=[
            pl.BlockSpec((1, chunk, WA), lambda i, c: (i, c, 0)),
            pl.BlockSpec((1, HEAD_PAIRS, LANES, LANES), lambda i, c: (i, 0, 0, 0)),
        ],
        out_shape=[
            jax.ShapeDtypeStruct((b, t, WA), BF16),
            jax.ShapeDtypeStruct((b, HEAD_PAIRS, LANES, LANES), F32),
        ],
        scratch_shapes=[pltpu.VMEM((1, PA), F32), pltpu.VMEM((HEAD_PAIRS, LANES, LANES), F32)],
        compiler_params=pltpu.CompilerParams(
            dimension_semantics=("parallel", "arbitrary"), vmem_limit_bytes=VMEM_LIMIT),
        name="rwkv7",
    )(pa3, shift0.reshape(b, 1, PA), _pair_states(s0), *params)
    return y, _unpair_states(s_out)


def _hgrn_body(q_ref, f_ref, i_ref, g_ref, s0_ref, lb_ref, nw_ref, y_ref, sout_ref, s_scr, *, chunk):
    c = pl.program_id(1)

    @pl.when(c == 0)
    def _():
        s_scr[...] = s0_ref[0]

    q = q_ref[0]
    fl = f_ref[0]
    iv = i_ref[0]
    g = g_ref[0]
    lb = lb_ref[...]
    f = lb + (1.0 - lb) * _sigmoid(fl)
    log_f = jnp.log(jnp.maximum(f, F_FLOOR))
    key = (1.0 - lb) * _sigmoid(-fl)

    ti = lax.broadcasted_iota(jnp.int32, (chunk, chunk), 0)
    si = lax.broadcasted_iota(jnp.int32, (chunk, chunk), 1)
    incl = si <= ti
    tri = jnp.where(incl, 1.0, 0.0).astype(F32)
    cum = _dot(tri, log_f, HIGHEST)
    mid = (chunk - 1) // 2
    anchor = cum[mid:mid + 1, :]
    last = cum[chunk - 1:chunk, :]
    qa = q * jnp.exp(cum - anchor)
    ka = key * jnp.exp(anchor - cum)
    qe = q * jnp.exp(cum)
    kl = key * jnp.exp(last - cum)

    outs = []
    for h in range(HC):
        sl = slice(h * LANES, (h + 1) * LANES)
        s = s_scr[h]
        att = jnp.where(incl, _dot_nt(qa[:, sl], ka[:, sl], HIGHEST), 0.0)
        o = _dot(att, iv[:, sl], HIGHEST) + _dot(qe[:, sl], s, HIGHEST)
        decay = jnp.exp(jnp.broadcast_to(last[:, sl], (DKC, LANES)).T)
        s_scr[h] = decay * s + _dot_tn(kl[:, sl], iv[:, sl], HIGHEST)
        ms = jnp.mean(o * o, axis=-1, keepdims=True)
        outs.append(o * lax.rsqrt(ms + 1e-6) * nw_ref[...])
    o = jnp.concatenate(outs, axis=1)
    y_ref[0] = (o * (g * _sigmoid(g))).astype(y_ref.dtype)

    @pl.when(c == pl.num_programs(1) - 1)
    def _():
        sout_ref[0] = s_scr[...]


def _hgrn(pc3, s0, lb, norm_w, layer, chunk):
    b, t, _ = pc3.shape
    nc = t // chunk
    col = lambda n: pl.BlockSpec((1, chunk, WC), lambda i, c: (i, c, n))
    y, s_out = pl.pallas_call(
        functools.partial(_hgrn_body, chunk=chunk),
        grid=(b, nc),
        in_specs=[
            col(0), col(1), col(2), col(3),
            pl.BlockSpec((1, HC, DKC, DVC), lambda i, c: (i, 0, 0, 0)),
            pl.BlockSpec((None, 1, WC), lambda i, c: (layer, 0, 0)),
            pl.BlockSpec((None, 1, DVC), lambda i, c: (layer, 0, 0)),
        ],
        out_specs=[
            pl.BlockSpec((1, chunk, WC), lambda i, c: (i, c, 0)),
            pl.BlockSpec((1, HC, DKC, DVC), lambda i, c: (i, 0, 0, 0)),
        ],
        out_shape=[
            jax.ShapeDtypeStruct((b, t, WC), BF16),
            jax.ShapeDtypeStruct((b, HC, DKC, DVC), F32),
        ],
        scratch_shapes=[pltpu.VMEM((HC, DKC, DVC), F32)],
        compiler_params=pltpu.CompilerParams(
            dimension_semantics=("parallel", "arbitrary"), vmem_limit_bytes=VMEM_LIMIT),
        name="hgrn2",
    )(pc3, pc3, pc3, pc3, s0, lb.reshape(DEPTH, 1, WC), norm_w.reshape(DEPTH, 1, DVC))
    return y, s_out


def _split_maps(q, scale):
    lane = lax.broadcasted_iota(jnp.int32, (1, LANES), 1)
    q = q * scale
    return jnp.concatenate([jnp.where(lane < DKB, q, 0.0), jnp.where(lane < DKB, 0.0, q)], axis=0).astype(BF16)


def _subln(o, w, scale):
    ms = jnp.mean(o * o, axis=-1, keepdims=True)
    return o * lax.rsqrt(ms + 1e-5) * w * scale


def _attn_prompt_body(lam_ref, q_ref, k_ref, v_ref, w_ref, o_ref, *, tq, out_scale):
    qi = pl.program_id(2)
    lam = lam_ref[0]
    qq = _split_maps(q_ref[0], DKB ** -0.5)
    k = k_ref[0].astype(BF16)
    s = _dot_nt(qq, k)
    t = s.shape[1]
    qpos = qi * tq + lax.broadcasted_iota(jnp.int32, (tq, 1), 0)
    qpos = jnp.concatenate([qpos, qpos], axis=0)
    kpos = lax.broadcasted_iota(jnp.int32, (1, t), 1)
    s = jnp.where(kpos <= qpos, s, MASK_VALUE)
    e = jnp.exp(s - jnp.max(s, axis=-1, keepdims=True))
    p = e * (1.0 / jnp.sum(e, axis=-1, keepdims=True))
    att = p[:tq] - lam * p[tq:]
    o = _dot(att.astype(BF16), v_ref[0].astype(BF16))
    o_ref[0] = _subln(o, w_ref[...], out_scale).astype(o_ref.dtype)


def _attn_prompt(pb3, lam, subln_w, layer, out_scale):
    b, t, _ = pb3.shape
    tq = 344 if t % 344 == 0 else t
    return pl.pallas_call(
        functools.partial(_attn_prompt_body, tq=tq, out_scale=out_scale),
        grid=(b, HB, t // tq),
        in_specs=[
            pl.BlockSpec(memory_space=pltpu.SMEM),
            pl.BlockSpec((1, tq, LANES), lambda i, h, q: (i, q, h)),
            pl.BlockSpec((1, t, LANES), lambda i, h, q: (i, 0, HB + h)),
            pl.BlockSpec((1, t, LANES), lambda i, h, q: (i, 0, 2 * HB + h)),
            pl.BlockSpec((None, 1, DVB), lambda i, h, q: (layer, 0, 0)),
        ],
        out_specs=pl.BlockSpec((1, tq, LANES), lambda i, h, q: (i, q, h)),
        out_shape=jax.ShapeDtypeStruct((b, t, WB), BF16),
        compiler_params=pltpu.CompilerParams(
            dimension_semantics=("parallel", "parallel", "arbitrary"), vmem_limit_bytes=VMEM_LIMIT),
        name="diff_attn_prompt",
    )(lam, pb3, pb3, pb3, subln_w.reshape(DEPTH, 1, DVB))


def _attn_sample_body(pt_ref, lam_ref, *refs, n_pg, t, out_scale):
    del pt_ref
    q_ref, kn_ref, vn_ref, w_ref = refs[:4]
    k_refs = refs[4:4 + n_pg]
    v_refs = refs[4 + n_pg:4 + 2 * n_pg]
    o_ref, m_scr, l_scr, acc_scr = refs[4 + 2 * n_pg:]
    g = pl.program_id(1)
    lam = lam_ref[0]
    q = q_ref[0]

    @pl.when(g == 0)
    def _():
        qpos = lax.broadcasted_iota(jnp.int32, (t, 1), 0)
        qpos = jnp.concatenate([qpos, qpos], axis=0)
        kpos = lax.broadcasted_iota(jnp.int32, (1, t), 1)
        for h in range(HB):
            sl = slice(h * LANES, (h + 1) * LANES)
            qq = _split_maps(q[:, sl], DKB ** -0.5)
            s = _dot_nt(qq, kn_ref[0][:, sl].astype(BF16))
            s = jnp.where(kpos <= qpos, s, MASK_VALUE)
            m = jnp.max(s, axis=-1, keepdims=True)
            e = jnp.exp(s - m)
            m_scr[h] = m
            l_scr[h] = jnp.sum(e, axis=-1, keepdims=True)
            acc_scr[h] = _dot(e.astype(BF16), vn_ref[0][:, sl].astype(BF16))

    for h in range(HB):
        sl = slice(h * LANES, (h + 1) * LANES)
        qq = _split_maps(q[:, sl], DKB ** -0.5)
        s = jnp.concatenate([_dot_nt(qq, kr[:, sl].astype(BF16)) for kr in k_refs], axis=1)
        m_old = m_scr[h]
        m_new = jnp.maximum(m_old, jnp.max(s, axis=-1, keepdims=True))
        e = jnp.exp(s - m_new).astype(BF16)
        corr = jnp.exp(m_old - m_new)
        pv = _dot(e[:, :PAGE_SIZE], v_refs[0][:, sl].astype(BF16))
        for n in range(1, n_pg):
            pv = pv + _dot(e[:, n * PAGE_SIZE:(n + 1) * PAGE_SIZE], v_refs[n][:, sl].astype(BF16))
        l_scr[h] = corr * l_scr[h] + jnp.sum(e.astype(F32), axis=-1, keepdims=True)
        acc_scr[h] = corr * acc_scr[h] + pv
        m_scr[h] = m_new

    @pl.when(g == pl.num_programs(1) - 1)
    def _():
        outs = []
        for h in range(HB):
            o = acc_scr[h] * (1.0 / l_scr[h])
            outs.append(_subln(o[:t] - lam * o[t:], w_ref[...], out_scale))
        o_ref[0] = jnp.concatenate(outs, axis=1).astype(o_ref.dtype)


def _attn_sample(pb3, cache_k, cache_v, page_table, lam, subln_w, layer, out_scale, n_pg=8):
    b, t, _ = pb3.shape
    n_pages = page_table.shape[1]
    depth, n_pool = cache_k.shape[:2]
    ck = cache_k.reshape(depth, n_pool, PAGE_SIZE, HB * 2 * DKB)
    cv = cache_v.reshape(depth, n_pool, PAGE_SIZE, HB * DVB)

    def page_spec(n):
        return pl.BlockSpec((None, None, PAGE_SIZE, WB), lambda i, g, pt: (layer, pt[i, g * n_pg + n], 0, 0))

    grid_spec = pltpu.PrefetchScalarGridSpec(
        num_scalar_prefetch=1,
        grid=(b, n_pages // n_pg),
        in_specs=[
            pl.BlockSpec(memory_space=pltpu.SMEM),
            pl.BlockSpec((1, t, WB), lambda i, g, pt: (i, 0, 0)),
            pl.BlockSpec((1, t, WB), lambda i, g, pt: (i, 0, 1)),
            pl.BlockSpec((1, t, WB), lambda i, g, pt: (i, 0, 2)),
            pl.BlockSpec((None, 1, DVB), lambda i, g, pt: (layer, 0, 0)),
        ] + [page_spec(n) for n in range(n_pg)] * 2,
        out_specs=pl.BlockSpec((1, t, WB), lambda i, g, pt: (i, 0, 0)),
        scratch_shapes=[
            pltpu.VMEM((HB, 2 * t, 1), F32), pltpu.VMEM((HB, 2 * t, 1), F32), pltpu.VMEM((HB, 2 * t, DVB), F32)],
    )
    return pl.pallas_call(
        functools.partial(_attn_sample_body, n_pg=n_pg, t=t, out_scale=out_scale),
        grid_spec=grid_spec,
        out_shape=jax.ShapeDtypeStruct((b, t, WB), BF16),
        compiler_params=pltpu.CompilerParams(
            dimension_semantics=("parallel", "arbitrary"), vmem_limit_bytes=VMEM_LIMIT),
        name="diff_attn_sample",
    )(page_table, lam, pb3, pb3, pb3, subln_w.reshape(DEPTH, 1, DVB), *([ck] * n_pg), *([cv] * n_pg))


def _trunk(h3, paged, rwkv0, shift0, hgrn0, conv0, wt, chunk):
    b, t, _ = h3.shape
    m = b * t
    h = h3.reshape(m, D_MODEL)
    lbs = jax.nn.softmax(wt['hg_lb'].astype(F32), axis=0)
    lbs = jnp.cumsum(lbs, axis=0) - lbs[0]
    wa2 = jnp.concatenate([wt['rw_w2'], wt['rw_a2']], axis=1)
    row = lambda name, n: wt[name].reshape(DEPTH, 1, n)
    rw_params = (row('rw_mu', PA), row('rw_w0', WA), wa2, row('rw_a0', WA), wt['rw_g2'], row('rw_kk', WA),
                 row('rw_ka', WA), row('rw_rk', WA), row('rw_lnw', WA), row('rw_lnb', WA))
    ks, vs, sas, shs, scs, cvs = [], [], [], [], [], []
    for l in range(DEPTH):
        pa = _matmul(h, wt['w_in'], l, n_cols=PA, col_off=0, gain=wt['norm1'])
        pb = _matmul(h, wt['w_in'], l, n_cols=PB, col_off=PA // 512, gain=wt['norm1'])
        pc = _matmul(h, wt['w_in'], l, n_cols=PC, col_off=(PA + PB) // 512, gain=wt['norm1'])
        pa3 = pa.reshape(b, t, PA)
        pb3 = pb.reshape(b, t, PB)
        pc3 = pc.reshape(b, t, PC)
        ya, sa = _rwkv(pa3, shift0[l], rwkv0[l], rw_params, l, chunk)
        lam_init = 0.8 - 0.6 * math.exp(-0.3 * l)
        lam = (jnp.exp(jnp.sum(wt['da_lq1'][l].astype(F32) * wt['da_lk1'][l].astype(F32)))
               - jnp.exp(jnp.sum(wt['da_lq2'][l].astype(F32) * wt['da_lk2'][l].astype(F32))) + lam_init)
        lam = lam.reshape(1).astype(F32)
        if paged is None:
            yb = _attn_prompt(pb3, lam, wt['da_subln'], l, 1.0 - lam_init)
        else:
            yb = _attn_sample(pb3, paged[0], paged[1], paged[2], lam, wt['da_subln'], l, 1.0 - lam_init)
        yc, sc = _hgrn(pc3, hgrn0[l], lbs, wt['hg_norm'], l, chunk)
        mix = jnp.concatenate([ya, yb, yc], axis=-1).reshape(m, D_MODEL)
        h = _matmul(mix, wt['w_out'], l, n_cols=D_MODEL, residual=h)
        u = _matmul(h, wt['ffn_up'], l, n_cols=2 * D_FF, gain=wt['norm2'])
        u3 = u.reshape(b, t, 2 * D_FF)
        act = _conv_gate(u3, conv0, wt['ffn_conv'], wt['ffn_conv_b'], l)
        h = _matmul(act.reshape(m, D_FF), wt['ffn_down'], l, n_cols=D_MODEL, tn=256, residual=h)
        ks.append(pb3[:, :, WB:2 * WB].reshape(b, t, HB, 2 * DKB))
        vs.append(pb3[:, :, 2 * WB:].reshape(b, t, HB, DVB))
        sas.append(sa)
        shs.append(pa3[:, t - 1, :])
        scs.append(sc)
        cvs.append(u3[:, t - (CONV_W - 1):, :])
    y = _rmsnorm(h, wt['final_norm']).reshape(b, t, D_MODEL)
    return (y, jnp.stack(ks), jnp.stack(vs), jnp.stack(sas), jnp.stack(shs), jnp.stack(scs), jnp.stack(cvs))


def kernel(x_prompt, x_sample, cache_k, cache_v, state_rwkv, state_shift, state_hgrn, state_conv, page_table, meta, norm1, w_in, rw_mu, rw_w0, rw_w2, rw_a0, rw_a2, rw_g2, rw_kk, rw_ka, rw_rk, rw_lnw, rw_lnb, da_lq1, da_lk1, da_lq2, da_lk2, da_subln, hg_lb, hg_norm, w_out, norm2, ffn_up, ffn_conv, ffn_conv_b, ffn_down, final_norm):
    wt = {'norm1': norm1, 'w_in': w_in, 'rw_mu': rw_mu, 'rw_w0': rw_w0, 'rw_w2': rw_w2, 'rw_a0': rw_a0,
          'rw_a2': rw_a2, 'rw_g2': rw_g2, 'rw_kk': rw_kk, 'rw_ka': rw_ka, 'rw_rk': rw_rk, 'rw_lnw': rw_lnw,
          'rw_lnb': rw_lnb, 'da_lq1': da_lq1, 'da_lk1': da_lk1, 'da_lq2': da_lq2, 'da_lk2': da_lk2,
          'da_subln': da_subln, 'hg_lb': hg_lb, 'hg_norm': hg_norm, 'w_out': w_out, 'norm2': norm2,
          'ffn_up': ffn_up, 'ffn_conv': ffn_conv, 'ffn_conv_b': ffn_conv_b, 'ffn_down': ffn_down,
          'final_norm': final_norm}
    bp = x_prompt.shape[0]
    dt = x_prompt.dtype
    hp = jnp.concatenate([jnp.broadcast_to(meta[None].astype(dt), (bp, N_META, D_MODEL)), x_prompt], axis=1)
    tp = hp.shape[1]
    chunk_p = next(c for c in (48, 24, 16, 8) if tp % c == 0)
    yp, p_k, p_v, p_rwkv, p_shift, p_hgrn, p_conv = _trunk(
        hp, None,
        jnp.zeros((DEPTH, bp, HA, NA, NA), dt), jnp.zeros((DEPTH, bp, PA), dt),
        jnp.zeros((DEPTH, bp, HC, DKC, DVC), dt), jnp.zeros((DEPTH, bp, CONV_W - 1, 2 * D_FF), dt),
        wt, chunk_p)
    y_prompt = yp[:, N_META:]
    ts = x_sample.shape[1]
    y_sample, s_k, s_v, s_rwkv, s_shift, s_hgrn, s_conv = _trunk(
        x_sample, (cache_k, cache_v, page_table), state_rwkv, state_shift, state_hgrn, state_conv, wt, ts)
    return (y_prompt, y_sample, p_k, p_v, p_rwkv, p_shift, p_hgrn, p_conv,
            s_k, s_v, s_rwkv, s_shift, s_hgrn, s_conv)
```

```python
import functools
import math

import jax
import jax.numpy as jnp
from jax import lax
from jax.experimental import pallas as pl
from jax.experimental.pallas import tpu as pltpu

F32 = jnp.float32
BF16 = jnp.bfloat16

D_MODEL = 2048
DEPTH = 2
N_META = 16
PAGE_SIZE = 128
WB = D_MODEL // 4
WA = (D_MODEL - WB) // 2
WC = D_MODEL - WA - WB
NA = 64
HA = WA // NA
LORA_W = 64
LORA_A = 64
LORA_G = 128
DVB = 128
DKB = DVB // 2
HB = WB // DVB
MASK_VALUE = -1e30
DVC = 128
DKC = 128
HC = WC // DVC
F_FLOOR = 1e-30
D_FF = ((8 * D_MODEL // 3 + 255) // 256) * 256
CONV_W = 3
PA = 3 * WA + LORA_W + LORA_A + LORA_G
PB = 2 * HB * 2 * DKB + HB * DVB
PC = 2 * HC * DKC + 2 * WC

LANES = 128
HEAD_PAIRS = HA // 2
VMEM_LIMIT = 56 * 1024 * 1024


def _dot(a, b):
    return jnp.dot(a, b, preferred_element_type=F32)


def _dot_nt(a, b):
    return lax.dot_general(a, b, (((1,), (1,)), ((), ())), preferred_element_type=F32)


def _dot_tn(a, b):
    return lax.dot_general(a, b, (((0,), (0,)), ((), ())), preferred_element_type=F32)


def _bf16_pieces(x, n):
    pieces = []
    for _ in range(n - 1):
        p = x.astype(BF16)
        pieces.append(p)
        x = x - p.astype(F32)
    pieces.append(x.astype(BF16))
    return pieces


def _dot_exact_lhs(m16, x, n):
    ps = _bf16_pieces(x, n)
    acc = _dot(m16, ps[-1])
    for p in ps[-2::-1]:
        acc = acc + _dot(m16, p)
    return acc


def _dot_x3(a, b):
    ah, al = _bf16_pieces(a, 2)
    bh, bl = _bf16_pieces(b, 2)
    return _dot(ah, bh) + (_dot(al, bh) + _dot(ah, bl))


def _sigmoid(x):
    return 1.0 / (1.0 + jnp.exp(-x))


def _softplus(x):
    return jnp.maximum(x, 0.0) + jnp.log(1.0 + jnp.exp(-jnp.abs(x)))


def _mm_body(*refs, norm, eps, residual, sub):
    refs = list(refs)
    x_ref = refs.pop(0)
    g_ref = refs.pop(0) if norm else None
    w_ref = refs.pop(0)
    r_ref = refs.pop(0) if residual else None
    o_ref, xs_ref = refs

    @pl.when(pl.program_id(1) == 0)
    def _():
        tm = x_ref.shape[0]
        for s in range(0, tm, sub):
            x = x_ref[s:s + sub, :].astype(F32)
            if norm:
                ms = jnp.mean(x * x, axis=-1, keepdims=True)
                x = x * lax.rsqrt(ms + eps) * g_ref[...]
            xs_ref[s:s + sub, :] = x.astype(BF16)

    acc = _dot(xs_ref[...], w_ref[...].astype(BF16))
    if residual:
        acc = acc + r_ref[...]
    o_ref[...] = acc


def _row_tile(m):
    for t in (1032, 688, 512, 344, 256, 128, 64, 32, 16, 8):
        if m % t == 0:
            return t
    raise ValueError(m)


def _matmul(x, w, layer, *, n_cols, col_off=0, tn=512, gain=None, eps=1e-6, residual=None):
    m, k = x.shape
    tm = _row_tile(m)
    sub = 344 if tm % 344 == 0 else tm
    norm = gain is not None
    in_specs = [pl.BlockSpec((tm, k), lambda i, j: (i, 0))]
    args = [x]
    if norm:
        in_specs.append(pl.BlockSpec((None, 1, k), lambda i, j: (layer, 0, 0)))
        args.append(gain.reshape(gain.shape[0], 1, k))
    in_specs.append(pl.BlockSpec((None, k, tn), lambda i, j: (layer, 0, j + col_off)))
    args.append(w)
    if residual is not None:
        in_specs.append(pl.BlockSpec((tm, tn), lambda i, j: (i, j)))
        args.append(residual)
    return pl.pallas_call(
        functools.partial(_mm_body, norm=norm, eps=eps, residual=residual is not None, sub=sub),
        grid=(m // tm, n_cols // tn),
        in_specs=in_specs,
        out_specs=pl.BlockSpec((tm, tn), lambda i, j: (i, j)),
        out_shape=jax.ShapeDtypeStruct((m, n_cols), F32),
        scratch_shapes=[pltpu.VMEM((tm, k), BF16)],
        compiler_params=pltpu.CompilerParams(
            dimension_semantics=("parallel", "arbitrary"), vmem_limit_bytes=VMEM_LIMIT),
        name="matmul",
    )(*args)


def _norm_body(x_ref, g_ref, o_ref, *, eps):
    x = x_ref[...]
    ms = jnp.mean(x * x, axis=-1, keepdims=True)
    o_ref[...] = x * lax.rsqrt(ms + eps) * g_ref[...]


def _rmsnorm(x, gain, eps=1e-6):
    m, k = x.shape
    tm = 344 if m % 344 == 0 else _row_tile(m)
    return pl.pallas_call(
        functools.partial(_norm_body, eps=eps),
        grid=(m // tm,),
        in_specs=[pl.BlockSpec((tm, k), lambda i: (i, 0)), pl.BlockSpec((1, k), lambda i: (0, 0))],
        out_specs=pl.BlockSpec((tm, k), lambda i: (i, 0)),
        out_shape=jax.ShapeDtypeStruct((m, k), F32),
        compiler_params=pltpu.CompilerParams(dimension_semantics=("parallel",)),
        name="final_norm",
    )(x, gain.reshape(1, k))


def _conv_body(ug_ref, uv_ref, bg_ref, bv_ref, wg_ref, wv_ref, cg_ref, cv_ref, o_ref):
    def conv(u_ref, buf_ref, w_ref, c_ref):
        u = u_ref[...]
        buf = buf_ref[...]
        w = w_ref[...]
        row = lax.broadcasted_iota(jnp.int32, (1, u.shape[1], 1), 1)
        b0 = buf[:, 0:1, :]
        b1 = buf[:, 1:2, :]
        u1 = jnp.where(row == 0, b1, pltpu.roll(u, 1, 1))
        u2 = jnp.where(row == 0, b0, jnp.where(row == 1, b1, pltpu.roll(u, 2, 1)))
        return c_ref[...] + u2 * w[0:1, :] + u1 * w[1:2, :] + u * w[2:3, :]

    gate = conv(ug_ref, bg_ref, wg_ref, cg_ref)
    val = conv(uv_ref, bv_ref, wv_ref, cv_ref)
    o_ref[...] = (gate * _sigmoid(gate) * val).astype(o_ref.dtype)


def _conv_gate(u3, buf, w_conv, b_conv, layer, *, tn=512):
    b, t, _ = u3.shape
    bb = b if t <= 64 else 1
    nj = D_FF // tn
    b_conv3 = b_conv.reshape(DEPTH, 1, 2 * D_FF)
    blk = lambda off: pl.BlockSpec((bb, t, tn), lambda i, j: (i, 0, j + off))
    bufs = lambda off: pl.BlockSpec((None, bb, CONV_W - 1, tn), lambda i, j: (layer, i, 0, j + off))
    ws = lambda off: pl.BlockSpec((None, CONV_W, tn), lambda i, j: (layer, 0, j + off))
    cs = lambda off: pl.BlockSpec((None, 1, tn), lambda i, j: (layer, 0, j + off))
    return pl.pallas_call(
        _conv_body,
        grid=(b // bb, nj),
        in_specs=[blk(0), blk(nj), bufs(0), bufs(nj), ws(0), ws(nj), cs(0), cs(nj)],
        out_specs=pl.BlockSpec((bb, t, tn), lambda i, j: (i, 0, j)),
        out_shape=jax.ShapeDtypeStruct((b, t, D_FF), BF16),
        compiler_params=pltpu.CompilerParams(
            dimension_semantics=("parallel", "parallel"), vmem_limit_bytes=VMEM_LIMIT),
        name="conv_gate",
    )(u3, u3, buf, buf, w_conv, w_conv, b_conv3, b_conv3)


def _seg_sum64(x, bd16):
    n = x.shape[0]
    st = jnp.concatenate(_bf16_pieces(x, 2), axis=0)
    outs = []
    for j in range(x.shape[1] // LANES):
        y = _dot(st[:, j * LANES:(j + 1) * LANES], bd16)
        outs.append(y[:n] + y[n:])
    return jnp.concatenate(outs, axis=1)


def _rwkv_body(x_ref, shift_ref, s0_ref, mu_ref, w0_ref, wa2_ref, a0_ref, g2_ref, kk_ref, ka_ref, rk_ref,
               lnw_ref, lnb_ref, y_ref, sout_ref, prev_scr, s_scr, *, nb, chunk, n_sq):
    c = pl.program_id(1)
    n = nb * chunk

    @pl.when(c == 0)
    def _():
        prev_scr[...] = shift_ref[...]
        s_scr[...] = s0_ref[...]

    x = x_ref[...].reshape(n, PA)
    row = lax.broadcasted_iota(jnp.int32, (n, 1), 0)
    prev = pltpu.roll(x, 1, 0)
    for i in range(nb):
        prev = jnp.where(row == i * chunk, prev_scr[i], prev)
        prev_scr[i] = x[(i + 1) * chunk - 1:(i + 1) * chunk, :]
    xm = x + (prev - x) * mu_ref[...]
    r = xm[:, :WA]
    k = xm[:, WA:2 * WA]
    v = xm[:, 2 * WA:3 * WA]
    wa_in = xm[:, 3 * WA:3 * WA + LANES]
    gd = xm[:, 3 * WA + LANES:]

    lane = lax.broadcasted_iota(jnp.int32, (1, LANES), 1)
    lo = lane < NA
    wa2 = wa2_ref[...]
    z = w0_ref[...] + _dot_x3(jnp.where(lo, jnp.tanh(wa_in), 0.0), wa2)
    w = -_softplus(-z) - 0.5
    logd = -jnp.exp(w)
    a = _sigmoid(a0_ref[...] + _dot(jnp.where(lo, 0.0, wa_in).astype(BF16), wa2.astype(BF16)))
    g = _dot(_sigmoid(gd).astype(BF16), g2_ref[...].astype(BF16))

    ri = lax.broadcasted_iota(jnp.int32, (LANES, LANES), 0)
    ci = lax.broadcasted_iota(jnp.int32, (LANES, LANES), 1)
    bd_mask = (ri < NA) == (ci < NA)
    bd16 = jnp.where(bd_mask, 1.0, 0.0).astype(BF16)
    eye = ri == ci

    kkr = k * kk_ref[...]
    kk = kkr / jnp.maximum(jnp.sqrt(_seg_sum64(kkr * kkr, bd16)), 1e-12)
    k2 = k * (1.0 + (a - 1.0) * ka_ref[...])
    b = kk * a

    tn_ = lax.broadcasted_iota(jnp.int32, (n, n), 0)
    sn_ = lax.broadcasted_iota(jnp.int32, (n, n), 1)
    same = None
    for i in range(nb):
        blk = ((tn_ >= i * chunk) & (tn_ < (i + 1) * chunk) & (sn_ >= i * chunk) & (sn_ < (i + 1) * chunk))
        same = blk if same is None else (same | blk)
    tri_incl = jnp.where(same & (sn_ <= tn_), 1.0, 0.0).astype(BF16)
    tri_rest = jnp.where(same & (sn_ > tn_), 1.0, 0.0).astype(BF16)
    cl = _dot_exact_lhs(tri_incl, logd, 3)
    rl = _dot_exact_lhs(tri_rest, logd, 3)
    e_neg = jnp.exp(-cl)
    e_end = jnp.exp(rl)
    alpha = kk * jnp.exp(cl - logd)
    rho = r * jnp.exp(cl)
    beta = (b * e_neg).astype(BF16)
    kappa = (k2 * e_neg).astype(BF16)
    beta_e = (b * e_end).astype(BF16)
    kappa_e = (k2 * e_end).astype(BF16)
    v16 = v.astype(BF16)

    ti = lax.broadcasted_iota(jnp.int32, (chunk, chunk), 0)
    si = lax.broadcasted_iota(jnp.int32, (chunk, chunk), 1)
    incl = si <= ti
    strict = si < ti
    eye_c = jnp.where(ti == si, 1.0, 0.0).astype(F32)

    rows = lambda i: slice(i * chunk, (i + 1) * chunk)
    lanes = lambda j: slice(j * LANES, (j + 1) * LANES)
    pairs = [(i, j) for i in range(nb) for j in range(HEAD_PAIRS)]
    heads = [(i, j, hh) for (i, j) in pairs for hh in (0, 1)]

    a_p = {p: alpha[rows(p[0]), lanes(p[1])] for p in pairs}
    r_p = {p: rho[rows(p[0]), lanes(p[1])] for p in pairs}
    b_p = {p: beta[rows(p[0]), lanes(p[1])] for p in pairs}
    k_p = {p: kappa[rows(p[0]), lanes(p[1])] for p in pairs}
    v_p = {p: v16[rows(p[0]), lanes(p[1])] for p in pairs}
    ar_p = {p: jnp.concatenate([a_p[p], r_p[p]], axis=0) for p in pairs}

    arm = {}
    for (i, j, hh) in heads:
        m = lo if hh == 0 else jnp.logical_not(lo)
        arm[(i, j, hh)] = jnp.where(m, ar_p[(i, j)], 0.0).astype(BF16)
    mb = {h: _dot_nt(arm[h], b_p[h[:2]]) for h in heads}
    mk = {h: _dot_nt(arm[h], k_p[h[:2]]) for h in heads}
    mab = {h: jnp.where(strict, mb[h][:chunk], 0.0) for h in heads}
    mrb = {h: jnp.where(incl, mb[h][chunk:], 0.0).astype(BF16) for h in heads}
    mak = {h: jnp.where(strict, mk[h][:chunk], 0.0).astype(BF16) for h in heads}
    mrk = {h: jnp.where(incl, mk[h][chunk:], 0.0).astype(BF16) for h in heads}
    xh = {h: _dot(mak[h], v_p[h[:2]]) for h in heads}
    ov = {h: _dot(mrk[h], v_p[h[:2]]) for h in heads}
    pw = {h: (-mab[h]).astype(BF16) for h in heads}
    tinv = {h: eye_c - mab[h] for h in heads}
    for _ in range(n_sq):
        pw = {h: _dot(pw[h], pw[h]).astype(BF16) for h in heads}
        tinv = {h: tinv[h] + _dot(tinv[h].astype(BF16), pw[h]) for h in heads}
    wu = {h: _dot(tinv[h].astype(BF16), jnp.concatenate([a_p[h[:2]], xh[h]], axis=1).astype(BF16))
          for h in heads}
    y2 = {h: _dot(mrb[h], wu[h].astype(BF16)) for h in heads}

    def merge(f):
        return {p: jnp.where(lo, f((p[0], p[1], 0)), f((p[0], p[1], 1))) for p in pairs}

    w2 = merge(lambda h: wu[h][:, :LANES])
    u0 = merge(lambda h: wu[h][:, LANES:])
    rw = merge(lambda h: r_p[h[:2]] - y2[h][:, :LANES])
    o0 = merge(lambda h: ov[h] - y2[h][:, LANES:])

    s_old = {p: s_scr[p[0], p[1]] for p in pairs}
    s16 = {p: s_old[p].astype(BF16) for p in pairs}
    out = {p: _dot_nt(rw[p].astype(BF16), s16[p]) + o0[p] for p in pairs}
    w16 = {p: w2[p].astype(BF16) for p in pairs}
    be_p = {p: beta_e[rows(p[0]), lanes(p[1])] for p in pairs}
    ke_p = {p: kappa_e[rows(p[0]), lanes(p[1])] for p in pairs}
    wtb = {p: _dot_tn(w16[p], be_p[p]) for p in pairs}
    gm = {p: _dot_tn(jnp.concatenate([v_p[p], (-u0[p]).astype(BF16)], axis=0),
                     jnp.concatenate([ke_p[p], be_p[p]], axis=0)) for p in pairs}
    for p in pairs:
        i, j = p
        p_end = jnp.exp(cl[(i + 1) * chunk - 1:(i + 1) * chunk, lanes(j)])
        phi = jnp.where(bd_mask, jnp.where(eye, p_end, 0.0) - wtb[p], 0.0)
        s_scr[i, j] = _dot(s16[p], phi.astype(BF16)) + jnp.where(bd_mask, gm[p], 0.0)
    o = jnp.concatenate(
        [jnp.concatenate([out[(i, j)] for j in range(HEAD_PAIRS)], axis=1) for i in range(nb)], axis=0)

    mean = _seg_sum64(o, bd16) * (1.0 / NA)
    oc = o - mean
    var = _seg_sum64(oc * oc, bd16) * (1.0 / NA)
    o = oc * lax.rsqrt(var + 64e-5) * lnw_ref[...] + lnb_ref[...]
    o = o + _seg_sum64(r * k2 * rk_ref[...], bd16) * v
    y_ref[...] = (o * g).reshape(nb, chunk, WA).astype(y_ref.dtype)

    @pl.when(c == pl.num_programs(1) - 1)
    def _():
        sout_ref[...] = s_scr[...]


def _n_squarings(chunk):
    n, reach = 0, 1
    while reach < chunk - 1:
        n += 1
        reach = 2 * reach + 1
    return n


def _pair_states(s):
    b = s.shape[0]
    s = s.reshape(b, HEAD_PAIRS, 2, NA, NA)
    z = jnp.zeros((b, HEAD_PAIRS, NA, NA), s.dtype)
    top = jnp.concatenate([s[:, :, 0], z], axis=-1)
    bot = jnp.concatenate([z, s[:, :, 1]], axis=-1)
    return jnp.concatenate([top, bot], axis=-2)


def _unpair_states(sp):
    b = sp.shape[0]
    s = jnp.stack([sp[:, :, :NA, :NA], sp[:, :, NA:, NA:]], axis=2)
    return s.reshape(b, HA, NA, NA)


def _rwkv(pa3, shift0, s0, params, layer, chunk, nb):
    b, t, _ = pa3.shape
    nc = t // chunk
    vec = lambda n: pl.BlockSpec((None, 1, n), lambda i, c: (layer, 0, 0))
    mat = lambda r, n: pl.BlockSpec((None, r, n), lambda i, c: (layer, 0, 0))
    y, s_out = pl.pallas_call(
        functools.partial(_rwkv_body, nb=nb, chunk=chunk, n_sq=_n_squarings(chunk)),
        grid=(b // nb, nc),
        in_specs=[
            pl.BlockSpec((nb, chunk, PA), lambda i, c: (i, c, 0)),
            pl.BlockSpec((nb, 1, PA), lambda i, c: (i, 0, 0)),
            pl.BlockSpec((nb, HEAD_PAIRS, LANES, LANES), lambda i, c: (i, 0, 0, 0)),
            vec(PA), vec(WA), mat(LANES, WA), vec(WA), mat(LORA_G, WA), vec(WA), vec(WA), vec(WA), vec(WA), vec(WA),
        ],
        out_specs=[
            pl.BlockSpec((nb, chunk, WA), lambda i, c: (i, c, 0)),
            pl.BlockSpec((nb, HEAD_PAIRS, LANES, LANES), lambda i, c: (i, 0, 0, 0)),
        ],
        out_shape=[
            jax.ShapeDtypeStruct((b, t, WA), BF16),
            jax.ShapeDtypeStruct((b, HEAD_PAIRS, LANES, LANES), F32),
        ],
        scratch_shapes=[pltpu.VMEM((nb, 1, PA), F32), pltpu.VMEM((nb, HEAD_PAIRS, LANES, LANES), F32)],
        compiler_params=pltpu.CompilerParams(
            dimension_semantics=("parallel", "arbitrary"), vmem_limit_bytes=VMEM_LIMIT),
        name="rwkv7",
    )(pa3, shift0.reshape(b, 1, PA), _pair_states(s0), *params)
    return y, _unpair_states(s_out)


def _hgrn_body(q_ref, f_ref, i_ref, g_ref, s0_ref, lb_ref, nw_ref, y_ref, sout_ref, s_scr, *, chunk):
    c = pl.program_id(1)

    @pl.when(c == 0)
    def _():
        s_scr[...] = s0_ref[0]

    q = q_ref[0]
    fl = f_ref[0]
    iv = i_ref[0].astype(BF16)
    g = g_ref[0]
    lb = lb_ref[...]
    f = lb + (1.0 - lb) * _sigmoid(fl)
    log_f = jnp.log(jnp.maximum(f, F_FLOOR))
    key = (1.0 - lb) * _sigmoid(-fl)

    ti = lax.broadcasted_iota(jnp.int32, (chunk, chunk), 0)
    si = lax.broadcasted_iota(jnp.int32, (chunk, chunk), 1)
    incl = si <= ti
    cum = _dot_exact_lhs(jnp.where(incl, 1.0, 0.0).astype(BF16), log_f, 3)
    mid = (chunk - 1) // 2
    anchor = cum[mid:mid + 1, :]
    last = cum[chunk - 1:chunk, :]
    qa = (q * jnp.exp(cum - anchor)).astype(BF16)
    ka = (key * jnp.exp(anchor - cum)).astype(BF16)
    qe = (q * jnp.exp(cum)).astype(BF16)
    kl = (key * jnp.exp(last - cum)).astype(BF16)

    hs = range(HC)
    sl = lambda h: slice(h * LANES, (h + 1) * LANES)
    s_old = {h: s_scr[h] for h in hs}
    att = {h: jnp.where(incl, _dot_nt(qa[:, sl(h)], ka[:, sl(h)]), 0.0).astype(BF16) for h in hs}
    inter = {h: _dot(qe[:, sl(h)], s_old[h].astype(BF16)) for h in hs}
    upd = {h: _dot_tn(kl[:, sl(h)], iv[:, sl(h)]) for h in hs}
    outs = []
    for h in hs:
        o = _dot(att[h], iv[:, sl(h)]) + inter[h]
        decay = jnp.exp(jnp.broadcast_to(last[:, sl(h)], (DKC, LANES)).T)
        s_scr[h] = decay * s_old[h] + upd[h]
        ms = jnp.mean(o * o, axis=-1, keepdims=True)
        outs.append(o * lax.rsqrt(ms + 1e-6) * nw_ref[...])
    o = jnp.concatenate(outs, axis=1)
    y_ref[0] = (o * (g * _sigmoid(g))).astype(y_ref.dtype)

    @pl.when(c == pl.num_programs(1) - 1)
    def _():
        sout_ref[0] = s_scr[...]


def _hgrn(pc3, s0, lb, norm_w, layer, chunk):
    b, t, _ = pc3.shape
    nc = t // chunk
    col = lambda n: pl.BlockSpec((1, chunk, WC), lambda i, c: (i, c, n))
    y, s_out = pl.pallas_call(
        functools.partial(_hgrn_body, chunk=chunk),
        grid=(b, nc),
        in_specs=[
            col(0), col(1), col(2), col(3),
            pl.BlockSpec((1, HC, DKC, DVC), lambda i, c: (i, 0, 0, 0)),
            pl.BlockSpec((None, 1, WC), lambda i, c: (layer, 0, 0)),
            pl.BlockSpec((None, 1, DVC), lambda i, c: (layer, 0, 0)),
        ],
        out_specs=[
            pl.BlockSpec((1, chunk, WC), lambda i, c: (i, c, 0)),
            pl.BlockSpec((1, HC, DKC, DVC), lambda i, c: (i, 0, 0, 0)),
        ],
        out_shape=[
            jax.ShapeDtypeStruct((b, t, WC), BF16),
            jax.ShapeDtypeStruct((b, HC, DKC, DVC), F32),
        ],
        scratch_shapes=[pltpu.VMEM((HC, DKC, DVC), F32)],
        compiler_params=pltpu.CompilerParams(
            dimension_semantics=("parallel", "arbitrary"), vmem_limit_bytes=VMEM_LIMIT),
        name="hgrn2",
    )(pc3, pc3, pc3, pc3, s0, lb.reshape(DEPTH, 1, WC), norm_w.reshape(DEPTH, 1, DVC))
    return y, s_out


def _split_maps(q, scale):
    lane = lax.broadcasted_iota(jnp.int32, (1, LANES), 1)
    q = q * scale
    return jnp.concatenate([jnp.where(lane < DKB, q, 0.0), jnp.where(lane < DKB, 0.0, q)], axis=0).astype(BF16)


def _subln(o, w, scale):
    ms = jnp.mean(o * o, axis=-1, keepdims=True)
    return o * lax.rsqrt(ms + 1e-5) * w * scale


def _attn_prompt_body(lam_ref, q_ref, k_ref, v_ref, w_ref, o_ref, *, tq, out_scale):
    qi = pl.program_id(2)
    lam = lam_ref[0]
    qq = _split_maps(q_ref[0], DKB ** -0.5)
    k = k_ref[0].astype(BF16)
    s = _dot_nt(qq, k)
    t = s.shape[1]
    qpos = qi * tq + lax.broadcasted_iota(jnp.int32, (tq, 1), 0)
    qpos = jnp.concatenate([qpos, qpos], axis=0)
    kpos = lax.broadcasted_iota(jnp.int32, (1, t), 1)
    s = jnp.where(kpos <= qpos, s, MASK_VALUE)
    e = jnp.exp(s - jnp.max(s, axis=-1, keepdims=True))
    p = e * (1.0 / jnp.sum(e, axis=-1, keepdims=True))
    att = p[:tq] - lam * p[tq:]
    o = _dot(att.astype(BF16), v_ref[0].astype(BF16))
    o_ref[0] = _subln(o, w_ref[...], out_scale).astype(o_ref.dtype)


def _attn_prompt(pb3, lam, subln_w, layer, out_scale):
    b, t, _ = pb3.shape
    tq = 344 if t % 344 == 0 else t
    return pl.pallas_call(
        functools.partial(_attn_prompt_body, tq=tq, out_scale=out_scale),
        grid=(b, HB, t // tq),
        in_specs=[
            pl.BlockSpec(memory_space=pltpu.SMEM),
            pl.BlockSpec((1, tq, LANES), lambda i, h, q: (i, q, h)),
            pl.BlockSpec((1, t, LANES), lambda i, h, q: (i, 0, HB + h)),
            pl.BlockSpec((1, t, LANES), lambda i, h, q: (i, 0, 2 * HB + h)),
            pl.BlockSpec((None, 1, DVB), lambda i, h, q: (layer, 0, 0)),
        ],
        out_specs=pl.BlockSpec((1, tq, LANES), lambda i, h, q: (i, q, h)),
        out_shape=jax.ShapeDtypeStruct((b, t, WB), BF16),
        compiler_params=pltpu.CompilerParams(
            dimension_semantics=("parallel", "parallel", "arbitrary"), vmem_limit_bytes=VMEM_LIMIT),
        name="diff_attn_prompt",
    )(lam, pb3, pb3, pb3, subln_w.reshape(DEPTH, 1, DVB))


def _attn_sample_body(pt_ref, lam_ref, *refs, n_pg, t, out_scale):
    del pt_ref
    q_ref, kn_ref, vn_ref, w_ref = refs[:4]
    k_refs = refs[4:4 + n_pg]
    v_refs = refs[4 + n_pg:4 + 2 * n_pg]
    o_ref, qq_scr, m_scr, l_scr, acc_scr = refs[4 + 2 * n_pg:]
    g = pl.program_id(1)
    hr = 2 * t
    page_rows = PAGE_SIZE * HB

    @pl.when(g == 0)
    def _():
        q = q_ref[0]
        qpos = lax.broadcasted_iota(jnp.int32, (t, 1), 0)
        qpos = jnp.concatenate([qpos, qpos], axis=0)
        kpos = lax.broadcasted_iota(jnp.int32, (1, t), 1)
        for h in range(HB):
            sl = slice(h * LANES, (h + 1) * LANES)
            rs = slice(h * hr, (h + 1) * hr)
            qq = _split_maps(q[:, sl], DKB ** -0.5)
            qq_scr[rs, :] = qq
            s = _dot_nt(qq, kn_ref[0][:, sl].astype(BF16))
            s = jnp.where(kpos <= qpos, s, MASK_VALUE)
            m = jnp.max(s, axis=-1, keepdims=True)
            e = jnp.exp(s - m)
            m_scr[rs, :] = m
            l_scr[rs, :] = jnp.sum(e, axis=-1, keepdims=True)
            acc_scr[rs, :] = _dot(e.astype(BF16), vn_ref[0][:, sl].astype(BF16))

    qq = qq_scr[...]
    rid = lax.broadcasted_iota(jnp.int32, (HB * hr, 1), 0)
    row_head = sum((rid >= h * hr).astype(jnp.int32) for h in range(1, HB))
    col_head = lax.broadcasted_iota(jnp.int32, (1, page_rows), 1) & (HB - 1)
    own = row_head == col_head
    s = jnp.concatenate(
        [jnp.where(own, _dot_nt(qq, kr[...].astype(BF16)), MASK_VALUE) for kr in k_refs], axis=1)
    m_old = m_scr[...]
    m_new = jnp.maximum(m_old, jnp.max(s, axis=-1, keepdims=True))
    e = jnp.exp(s - m_new).astype(BF16)
    corr = jnp.exp(m_old - m_new)
    pv = _dot(e[:, :page_rows], v_refs[0][...].astype(BF16))
    for n in range(1, n_pg):
        pv = pv + _dot(e[:, n * page_rows:(n + 1) * page_rows], v_refs[n][...].astype(BF16))
    l_scr[...] = corr * l_scr[...] + jnp.sum(e.astype(F32), axis=-1, keepdims=True)
    acc_scr[...] = corr * acc_scr[...] + pv
    m_scr[...] = m_new

    @pl.when(g == pl.num_programs(1) - 1)
    def _():
        lam = lam_ref[0]
        o = acc_scr[...] * (1.0 / l_scr[...])
        outs = []
        for h in range(HB):
            oh = o[h * hr:h * hr + t] - lam * o[h * hr + t:(h + 1) * hr]
            outs.append(_subln(oh, w_ref[...], out_scale))
        o_ref[0] = jnp.concatenate(outs, axis=1).astype(o_ref.dtype)


def _attn_sample(pb3, cache_k, cache_v, page_table, lam, subln_w, layer, out_scale, n_pg=8):
    b, t, _ = pb3.shape
    n_pages = page_table.shape[1]
    depth, n_pool = cache_k.shape[:2]
    assert HB & (HB - 1) == 0 and 2 * DKB == DVB == LANES
    ck = cache_k.reshape(depth, n_pool, PAGE_SIZE * HB, 2 * DKB)
    cv = cache_v.reshape(depth, n_pool, PAGE_SIZE * HB, DVB)

    def page_spec(n):
        return pl.BlockSpec((None, None, PAGE_SIZE * HB, LANES),
                            lambda i, g, pt: (layer, pt[i, g * n_pg + n], 0, 0))

    rows = HB * 2 * t
    grid_spec = pltpu.PrefetchScalarGridSpec(
        num_scalar_prefetch=1,
        grid=(b, n_pages // n_pg),
        in_specs=[
            pl.BlockSpec(memory_space=pltpu.SMEM),
            pl.BlockSpec((1, t, WB), lambda i, g, pt: (i, 0, 0)),
            pl.BlockSpec((1, t, WB), lambda i, g, pt: (i, 0, 1)),
            pl.BlockSpec((1, t, WB), lambda i, g, pt: (i, 0, 2)),
            pl.BlockSpec((None, 1, DVB), lambda i, g, pt: (layer, 0, 0)),
        ] + [page_spec(n) for n in range(n_pg)] * 2,
        out_specs=pl.BlockSpec((1, t, WB), lambda i, g, pt: (i, 0, 0)),
        scratch_shapes=[
            pltpu.VMEM((rows, LANES), BF16), pltpu.VMEM((rows, 1), F32), pltpu.VMEM((rows, 1), F32),
            pltpu.VMEM((rows, DVB), F32)],
    )
    return pl.pallas_call(
        functools.partial(_attn_sample_body, n_pg=n_pg, t=t, out_scale=out_scale),
        grid_spec=grid_spec,
        out_shape=jax.ShapeDtypeStruct((b, t, WB), BF16),
        compiler_params=pltpu.CompilerParams(
            dimension_semantics=("parallel", "arbitrary"), vmem_limit_bytes=VMEM_LIMIT),
        name="diff_attn_sample",
    )(page_table, lam, pb3, pb3, pb3, subln_w.reshape(DEPTH, 1, DVB), *([ck] * n_pg), *([cv] * n_pg))


def _trunk(h3, paged, rwkv0, shift0, hgrn0, conv0, wt, chunk, nb):
    b, t, _ = h3.shape
    m = b * t
    h = h3.reshape(m, D_MODEL)
    lbs = jax.nn.softmax(wt['hg_lb'].astype(F32), axis=0)
    lbs = jnp.cumsum(lbs, axis=0) - lbs[0]
    wa2 = jnp.concatenate([wt['rw_w2'], wt['rw_a2']], axis=1)
    row = lambda name, n: wt[name].reshape(DEPTH, 1, n)
    rw_params = (row('rw_mu', PA), row('rw_w0', WA), wa2, row('rw_a0', WA), wt['rw_g2'], row('rw_kk', WA),
                 row('rw_ka', WA), row('rw_rk', WA), row('rw_lnw', WA), row('rw_lnb', WA))
    ks, vs, sas, shs, scs, cvs = [], [], [], [], [], []
    for l in range(DEPTH):
        pa = _matmul(h, wt['w_in'], l, n_cols=PA, col_off=0, gain=wt['norm1'])
        pb = _matmul(h, wt['w_in'], l, n_cols=PB, col_off=PA // 512, gain=wt['norm1'])
        pc = _matmul(h, wt['w_in'], l, n_cols=PC, col_off=(PA + PB) // 512, gain=wt['norm1'])
        pa3 = pa.reshape(b, t, PA)
        pb3 = pb.reshape(b, t, PB)
        pc3 = pc.reshape(b, t, PC)
        ya, sa = _rwkv(pa3, shift0[l], rwkv0[l], rw_params, l, chunk, nb)
        lam_init = 0.8 - 0.6 * math.exp(-0.3 * l)
        lam = (jnp.exp(jnp.sum(wt['da_lq1'][l].astype(F32) * wt['da_lk1'][l].astype(F32)))
               - jnp.exp(jnp.sum(wt['da_lq2'][l].astype(F32) * wt['da_lk2'][l].astype(F32))) + lam_init)
        lam = lam.reshape(1).astype(F32)
        if paged is None:
            yb = _attn_prompt(pb3, lam, wt['da_subln'], l, 1.0 - lam_init)
        else:
            yb = _attn_sample(pb3, paged[0], paged[1], paged[2], lam, wt['da_subln'], l, 1.0 - lam_init)
        yc, sc = _hgrn(pc3, hgrn0[l], lbs, wt['hg_norm'], l, chunk)
        mix = jnp.concatenate([ya, yb, yc], axis=-1).reshape(m, D_MODEL)
        h = _matmul(mix, wt['w_out'], l, n_cols=D_MODEL, residual=h)
        u = _matmul(h, wt['ffn_up'], l, n_cols=2 * D_FF, gain=wt['norm2'])
        u3 = u.reshape(b, t, 2 * D_FF)
        act = _conv_gate(u3, conv0, wt['ffn_conv'], wt['ffn_conv_b'], l)
        h = _matmul(act.reshape(m, D_FF), wt['ffn_down'], l, n_cols=D_MODEL, tn=256, residual=h)
        ks.append(pb3[:, :, WB:2 * WB].reshape(b, t, HB, 2 * DKB))
        vs.append(pb3[:, :, 2 * WB:].reshape(b, t, HB, DVB))
        sas.append(sa)
        shs.append(pa3[:, t - 1, :])
        scs.append(sc)
        cvs.append(u3[:, t - (CONV_W - 1):, :])
    y = _rmsnorm(h, wt['final_norm']).reshape(b, t, D_MODEL)
    return (y, jnp.stack(ks), jnp.stack(vs), jnp.stack(sas), jnp.stack(shs), jnp.stack(scs), jnp.stack(cvs))


def kernel(x_prompt, x_sample, cache_k, cache_v, state_rwkv, state_shift, state_hgrn, state_conv, page_table, meta, norm1, w_in, rw_mu, rw_w0, rw_w2, rw_a0, rw_a2, rw_g2, rw_kk, rw_ka, rw_rk, rw_lnw, rw_lnb, da_lq1, da_lk1, da_lq2, da_lk2, da_subln, hg_lb, hg_norm, w_out, norm2, ffn_up, ffn_conv, ffn_conv_b, ffn_down, final_norm):
    wt = {'norm1': norm1, 'w_in': w_in, 'rw_mu': rw_mu, 'rw_w0': rw_w0, 'rw_w2': rw_w2, 'rw_a0': rw_a0,
          'rw_a2': rw_a2, 'rw_g2': rw_g2, 'rw_kk': rw_kk, 'rw_ka': rw_ka, 'rw_rk': rw_rk, 'rw_lnw': rw_lnw,
          'rw_lnb': rw_lnb, 'da_lq1': da_lq1, 'da_lk1': da_lk1, 'da_lq2': da_lq2, 'da_lk2': da_lk2,
          'da_subln': da_subln, 'hg_lb': hg_lb, 'hg_norm': hg_norm, 'w_out': w_out, 'norm2': norm2,
          'ffn_up': ffn_up, 'ffn_conv': ffn_conv, 'ffn_conv_b': ffn_conv_b, 'ffn_down': ffn_down,
          'final_norm': final_norm}
    bp = x_prompt.shape[0]
    dt = x_prompt.dtype
    hp = jnp.concatenate([jnp.broadcast_to(meta[None].astype(dt), (bp, N_META, D_MODEL)), x_prompt], axis=1)
    tp = hp.shape[1]
    chunk_p = next(c for c in (48, 24, 16, 8) if tp % c == 0)
    yp, p_k, p_v, p_rwkv, p_shift, p_hgrn, p_conv = _trunk(
        hp, None,
        jnp.zeros((DEPTH, bp, HA, NA, NA), dt), jnp.zeros((DEPTH, bp, PA), dt),
        jnp.zeros((DEPTH, bp, HC, DKC, DVC), dt), jnp.zeros((DEPTH, bp, CONV_W - 1, 2 * D_FF), dt),
        wt, chunk_p, 2 if bp % 2 == 0 else 1)
    y_prompt = yp[:, N_META:]
    bs, ts = x_sample.shape[:2]
    y_sample, s_k, s_v, s_rwkv, s_shift, s_hgrn, s_conv = _trunk(
        x_sample, (cache_k, cache_v, page_table), state_rwkv, state_shift, state_hgrn, state_conv, wt, ts,
        4 if bs % 4 == 0 else 1)
    return (y_prompt, y_sample, p_k, p_v, p_rwkv, p_shift, p_hgrn, p_conv,
            s_k, s_v, s_rwkv, s_shift, s_hgrn, s_conv)
```

```python
import functools
import math

import jax
import jax.numpy as jnp
from jax import lax
from jax.experimental import pallas as pl
from jax.experimental.pallas import tpu as pltpu

F32 = jnp.float32
BF16 = jnp.bfloat16

D_MODEL = 2048
DEPTH = 2
N_META = 16
PAGE_SIZE = 128
WB = D_MODEL // 4
WA = (D_MODEL - WB) // 2
WC = D_MODEL - WA - WB
NA = 64
HA = WA // NA
LORA_W = 64
LORA_A = 64
LORA_G = 128
DVB = 128
DKB = DVB // 2
HB = WB // DVB
MASK_VALUE = -1e30
DVC = 128
DKC = 128
HC = WC // DVC
F_FLOOR = 1e-30
D_FF = ((8 * D_MODEL // 3 + 255) // 256) * 256
CONV_W = 3
PA = 3 * WA + LORA_W + LORA_A + LORA_G
PB = 2 * HB * 2 * DKB + HB * DVB
PC = 2 * HC * DKC + 2 * WC

LANES = 128
HEAD_PAIRS = HA // 2
VMEM_LIMIT = 56 * 1024 * 1024


def _dot(a, b):
    return jnp.dot(a, b, preferred_element_type=F32)


def _dot_nt(a, b):
    return lax.dot_general(a, b, (((1,), (1,)), ((), ())), preferred_element_type=F32)


def _dot_tn(a, b):
    return lax.dot_general(a, b, (((0,), (0,)), ((), ())), preferred_element_type=F32)


def _bf16_pieces(x, n):
    pieces = []
    for _ in range(n - 1):
        p = x.astype(BF16)
        pieces.append(p)
        x = x - p.astype(F32)
    pieces.append(x.astype(BF16))
    return pieces


def _dot_exact_lhs(m16, x, n):
    ps = _bf16_pieces(x, n)
    acc = _dot(m16, ps[-1])
    for p in ps[-2::-1]:
        acc = acc + _dot(m16, p)
    return acc


def _dot_x3(a, b):
    ah, al = _bf16_pieces(a, 2)
    bh, bl = _bf16_pieces(b, 2)
    return _dot(ah, bh) + (_dot(al, bh) + _dot(ah, bl))


def _sigmoid(x):
    return 1.0 / (1.0 + jnp.exp(-x))


def _softplus(x):
    return jnp.maximum(x, 0.0) + jnp.log(1.0 + jnp.exp(-jnp.abs(x)))


def _mm_body(*refs, n_x, norm, eps, residual, sub):
    refs = list(refs)
    x_refs = [refs.pop(0) for _ in range(n_x)]
    g_ref = refs.pop(0) if norm else None
    w_ref = refs.pop(0)
    r_ref = refs.pop(0) if residual else None
    o_ref, xs_ref = refs

    @pl.when(pl.program_id(1) == 0)
    def _():
        tm = xs_ref.shape[0]
        col = 0
        for x_ref in x_refs:
            width = x_ref.shape[1]
            for s in range(0, tm, sub):
                x = x_ref[s:s + sub, :]
                if norm:
                    ms = jnp.mean(x * x, axis=-1, keepdims=True)
                    x = x * lax.rsqrt(ms + eps) * g_ref[...]
                xs_ref[s:s + sub, col:col + width] = x.astype(BF16)
            col += width

    acc = _dot(xs_ref[...], w_ref[...].astype(BF16))
    if residual:
        acc = acc + r_ref[...]
    o_ref[...] = acc


def _row_tile(m):
    for t in (1032, 688, 512, 344, 256, 128, 64, 32, 16, 8):
        if m % t == 0:
            return t
    raise ValueError(m)


def _matmul(x, w, layer, *, n_cols, col_off=0, tn=512, gain=None, eps=1e-6, residual=None):
    xs = x if isinstance(x, tuple) else (x,)
    m = xs[0].shape[0]
    k = sum(p.shape[1] for p in xs)
    tm = _row_tile(m)
    sub = 344 if tm % 344 == 0 else tm
    norm = gain is not None
    assert not norm or len(xs) == 1
    in_specs = [pl.BlockSpec((tm, p.shape[1]), lambda i, j: (i, 0)) for p in xs]
    args = list(xs)
    if norm:
        in_specs.append(pl.BlockSpec((None, 1, k), lambda i, j: (layer, 0, 0)))
        args.append(gain.reshape(gain.shape[0], 1, k))
    in_specs.append(pl.BlockSpec((None, k, tn), lambda i, j: (layer, 0, j + col_off)))
    args.append(w)
    if residual is not None:
        in_specs.append(pl.BlockSpec((tm, tn), lambda i, j: (i, j)))
        args.append(residual)
    return pl.pallas_call(
        functools.partial(_mm_body, n_x=len(xs), norm=norm, eps=eps, residual=residual is not None, sub=sub),
        grid=(m // tm, n_cols // tn),
        in_specs=in_specs,
        out_specs=pl.BlockSpec((tm, tn), lambda i, j: (i, j)),
        out_shape=jax.ShapeDtypeStruct((m, n_cols), F32),
        scratch_shapes=[pltpu.VMEM((tm, k), BF16)],
        compiler_params=pltpu.CompilerParams(
            dimension_semantics=("parallel", "arbitrary"), vmem_limit_bytes=VMEM_LIMIT),
        name="matmul",
    )(*args)


def _norm_body(x_ref, g_ref, o_ref, *, eps):
    x = x_ref[...]
    ms = jnp.mean(x * x, axis=-1, keepdims=True)
    o_ref[...] = x * lax.rsqrt(ms + eps) * g_ref[...]


def _rmsnorm(x, gain, eps=1e-6):
    m, k = x.shape
    tm = 344 if m % 344 == 0 else _row_tile(m)
    return pl.pallas_call(
        functools.partial(_norm_body, eps=eps),
        grid=(m // tm,),
        in_specs=[pl.BlockSpec((tm, k), lambda i: (i, 0)), pl.BlockSpec((1, k), lambda i: (0, 0))],
        out_specs=pl.BlockSpec((tm, k), lambda i: (i, 0)),
        out_shape=jax.ShapeDtypeStruct((m, k), F32),
        compiler_params=pltpu.CompilerParams(dimension_semantics=("parallel",)),
        name="final_norm",
    )(x, gain.reshape(1, k))


def _conv_body(ug_ref, uv_ref, bg_ref, bv_ref, wg_ref, wv_ref, cg_ref, cv_ref, o_ref):
    def conv(u_ref, buf_ref, w_ref, c_ref):
        u = u_ref[...]
        buf = buf_ref[...]
        w = w_ref[...]
        row = lax.broadcasted_iota(jnp.int32, (1, u.shape[1], 1), 1)
        b0 = buf[:, 0:1, :]
        b1 = buf[:, 1:2, :]
        u1 = jnp.where(row == 0, b1, pltpu.roll(u, 1, 1))
        u2 = jnp.where(row == 0, b0, jnp.where(row == 1, b1, pltpu.roll(u, 2, 1)))
        return c_ref[...] + u2 * w[0:1, :] + u1 * w[1:2, :] + u * w[2:3, :]

    gate = conv(ug_ref, bg_ref, wg_ref, cg_ref)
    val = conv(uv_ref, bv_ref, wv_ref, cv_ref)
    o_ref[...] = (gate * _sigmoid(gate) * val).astype(o_ref.dtype)


def _conv_gate(u3, buf, w_conv, b_conv, layer, *, tn=512):
    b, t, _ = u3.shape
    bb = b if t <= 64 else 1
    nj = D_FF // tn
    b_conv3 = b_conv.reshape(DEPTH, 1, 2 * D_FF)
    blk = lambda off: pl.BlockSpec((bb, t, tn), lambda i, j: (i, 0, j + off))
    bufs = lambda off: pl.BlockSpec((None, bb, CONV_W - 1, tn), lambda i, j: (layer, i, 0, j + off))
    ws = lambda off: pl.BlockSpec((None, CONV_W, tn), lambda i, j: (layer, 0, j + off))
    cs = lambda off: pl.BlockSpec((None, 1, tn), lambda i, j: (layer, 0, j + off))
    return pl.pallas_call(
        _conv_body,
        grid=(b // bb, nj),
        in_specs=[blk(0), blk(nj), bufs(0), bufs(nj), ws(0), ws(nj), cs(0), cs(nj)],
        out_specs=pl.BlockSpec((bb, t, tn), lambda i, j: (i, 0, j)),
        out_shape=jax.ShapeDtypeStruct((b, t, D_FF), BF16),
        compiler_params=pltpu.CompilerParams(
            dimension_semantics=("parallel", "parallel"), vmem_limit_bytes=VMEM_LIMIT),
        name="conv_gate",
    )(u3, u3, buf, buf, w_conv, w_conv, b_conv3, b_conv3)


def _up_conv_body(x_ref, g_ref, wg_ref, wv_ref, bg_ref, bv_ref, cwg_ref, cwv_ref, cbg_ref, cbv_ref,
                  act_ref, sg_ref, sv_ref, xs_ref, carry_ref, *, eps, sub, tiles_per_seq):
    i = pl.program_id(0)
    j = pl.program_id(1)
    tm = x_ref.shape[0]

    @pl.when((i == 0) & (j == 0))
    def _():
        carry_ref[...] = jnp.zeros_like(carry_ref)

    @pl.when(j == 0)
    def _():
        for s in range(0, tm, sub):
            x = x_ref[s:s + sub, :]
            ms = jnp.mean(x * x, axis=-1, keepdims=True)
            xs_ref[s:s + sub, :] = (x * lax.rsqrt(ms + eps) * g_ref[...]).astype(BF16)

    first = (i % tiles_per_seq) == 0
    row = lax.broadcasted_iota(jnp.int32, (tm, 1), 0)

    def conv(w_ref, buf_ref, cw_ref, cb_ref, s_ref, slot):
        u = _dot(xs_ref[...], w_ref[...].astype(BF16))
        before = jnp.where(first, buf_ref[0], carry_ref[j, slot])
        tail = u[tm - (CONV_W - 1):, :]
        carry_ref[j, slot] = tail
        s_ref[0] = tail
        cw = cw_ref[...]
        u1 = jnp.where(row == 0, before[1:2, :], pltpu.roll(u, 1, 0))
        u2 = jnp.where(row == 0, before[0:1, :], jnp.where(row == 1, before[1:2, :], pltpu.roll(u, 2, 0)))
        return cb_ref[...] + u2 * cw[0:1, :] + u1 * cw[1:2, :] + u * cw[2:3, :]

    gate = conv(wg_ref, bg_ref, cwg_ref, cbg_ref, sg_ref, 0)
    val = conv(wv_ref, bv_ref, cwv_ref, cbv_ref, sv_ref, 1)
    act_ref[...] = (gate * _sigmoid(gate) * val).astype(act_ref.dtype)


def _up_conv_gate(h, gain, w_up, buf, w_conv, b_conv, layer, t, *, tn=512, eps=1e-6):
    m, k = h.shape
    b = m // t
    tm = _row_tile(t)
    tiles_per_seq = t // tm
    nj = D_FF // tn
    b_conv3 = b_conv.reshape(DEPTH, 1, 2 * D_FF)
    wsp = lambda off: pl.BlockSpec((None, k, tn), lambda i, j: (layer, 0, j + off))
    bufs = lambda off: pl.BlockSpec((None, 1, CONV_W - 1, tn), lambda i, j: (layer, i // tiles_per_seq, 0, j + off))
    cws = lambda off: pl.BlockSpec((None, CONV_W, tn), lambda i, j: (layer, 0, j + off))
    cbs = lambda off: pl.BlockSpec((None, 1, tn), lambda i, j: (layer, 0, j + off))
    tail_spec = pl.BlockSpec((1, CONV_W - 1, tn), lambda i, j: (i, 0, j))
    act, sg, sv = pl.pallas_call(
        functools.partial(_up_conv_body, eps=eps, sub=344 if tm % 344 == 0 else tm, tiles_per_seq=tiles_per_seq),
        grid=(m // tm, nj),
        in_specs=[
            pl.BlockSpec((tm, k), lambda i, j: (i, 0)),
            pl.BlockSpec((None, 1, k), lambda i, j: (layer, 0, 0)),
            wsp(0), wsp(nj), bufs(0), bufs(nj), cws(0), cws(nj), cbs(0), cbs(nj),
        ],
        out_specs=[pl.BlockSpec((tm, tn), lambda i, j: (i, j)), tail_spec, tail_spec],
        out_shape=[
            jax.ShapeDtypeStruct((m, D_FF), BF16),
            jax.ShapeDtypeStruct((m // tm, CONV_W - 1, D_FF), F32),
            jax.ShapeDtypeStruct((m // tm, CONV_W - 1, D_FF), F32),
        ],
        scratch_shapes=[pltpu.VMEM((tm, k), BF16), pltpu.VMEM((nj, 2, CONV_W - 1, tn), F32)],
        compiler_params=pltpu.CompilerParams(
            dimension_semantics=("arbitrary", "arbitrary"), vmem_limit_bytes=VMEM_LIMIT),
        name="up_conv_gate",
    )(h, gain.reshape(DEPTH, 1, k), w_up, w_up, buf, buf, w_conv, w_conv, b_conv3, b_conv3)
    seq_end = slice(tiles_per_seq - 1, None, tiles_per_seq)
    return act, jnp.concatenate([sg[seq_end], sv[seq_end]], axis=-1)


def _seg_sum64(x, bd16):
    n = x.shape[0]
    st = jnp.concatenate(_bf16_pieces(x, 2), axis=0)
    outs = []
    for j in range(x.shape[1] // LANES):
        y = _dot(st[:, j * LANES:(j + 1) * LANES], bd16)
        outs.append(y[:n] + y[n:])
    return jnp.concatenate(outs, axis=1)


def _rwkv_body(x_ref, shift_ref, s0_ref, mu_ref, w0_ref, wa2_ref, a0_ref, g2_ref, kk_ref, ka_ref, rk_ref,
               lnw_ref, lnb_ref, y_ref, sout_ref, prev_scr, s_scr, *, nb, chunk, n_sq):
    c = pl.program_id(1)
    n = nb * chunk

    @pl.when(c == 0)
    def _():
        prev_scr[...] = shift_ref[...]
        s_scr[...] = s0_ref[...]

    x = x_ref[...].reshape(n, PA)
    row = lax.broadcasted_iota(jnp.int32, (n, 1), 0)
    prev = pltpu.roll(x, 1, 0)
    for i in range(nb):
        prev = jnp.where(row == i * chunk, prev_scr[i], prev)
        prev_scr[i] = x[(i + 1) * chunk - 1:(i + 1) * chunk, :]
    xm = x + (prev - x) * mu_ref[...]
    r = xm[:, :WA]
    k = xm[:, WA:2 * WA]
    v = xm[:, 2 * WA:3 * WA]
    wa_in = xm[:, 3 * WA:3 * WA + LANES]
    gd = xm[:, 3 * WA + LANES:]

    lane = lax.broadcasted_iota(jnp.int32, (1, LANES), 1)
    lo = lane < NA
    wa2 = wa2_ref[...]
    z = w0_ref[...] + _dot_x3(jnp.where(lo, jnp.tanh(wa_in), 0.0), wa2)
    w = -_softplus(-z) - 0.5
    logd = -jnp.exp(w)
    a = _sigmoid(a0_ref[...] + _dot(jnp.where(lo, 0.0, wa_in).astype(BF16), wa2.astype(BF16)))
    g = _dot(_sigmoid(gd).astype(BF16), g2_ref[...].astype(BF16))

    ri = lax.broadcasted_iota(jnp.int32, (LANES, LANES), 0)
    ci = lax.broadcasted_iota(jnp.int32, (LANES, LANES), 1)
    bd_mask = (ri < NA) == (ci < NA)
    bd16 = jnp.where(bd_mask, 1.0, 0.0).astype(BF16)
    eye = ri == ci

    kkr = k * kk_ref[...]
    kk = kkr / jnp.maximum(jnp.sqrt(_seg_sum64(kkr * kkr, bd16)), 1e-12)
    k2 = k * (1.0 + (a - 1.0) * ka_ref[...])
    b = kk * a

    tn_ = lax.broadcasted_iota(jnp.int32, (n, n), 0)
    sn_ = lax.broadcasted_iota(jnp.int32, (n, n), 1)
    same = None
    for i in range(nb):
        blk = ((tn_ >= i * chunk) & (tn_ < (i + 1) * chunk) & (sn_ >= i * chunk) & (sn_ < (i + 1) * chunk))
        same = blk if same is None else (same | blk)
    tri_incl = jnp.where(same & (sn_ <= tn_), 1.0, 0.0).astype(BF16)
    tri_rest = jnp.where(same & (sn_ > tn_), 1.0, 0.0).astype(BF16)
    cl = _dot_exact_lhs(tri_incl, logd, 3)
    rl = _dot_exact_lhs(tri_rest, logd, 3)
    e_neg = jnp.exp(-cl)
    e_end = jnp.exp(rl)
    alpha = kk * jnp.exp(cl - logd)
    rho = r * jnp.exp(cl)
    beta = (b * e_neg).astype(BF16)
    kappa = (k2 * e_neg).astype(BF16)
    beta_e = (b * e_end).astype(BF16)
    kappa_e = (k2 * e_end).astype(BF16)
    v16 = v.astype(BF16)

    ti = lax.broadcasted_iota(jnp.int32, (chunk, chunk), 0)
    si = lax.broadcasted_iota(jnp.int32, (chunk, chunk), 1)
    incl = si <= ti
    strict = si < ti
    eye_c = jnp.where(ti == si, 1.0, 0.0).astype(F32)

    rows = lambda i: slice(i * chunk, (i + 1) * chunk)
    lanes = lambda j: slice(j * LANES, (j + 1) * LANES)
    pairs = [(i, j) for i in range(nb) for j in range(HEAD_PAIRS)]
    heads = [(i, j, hh) for (i, j) in pairs for hh in (0, 1)]

    a_p = {p: alpha[rows(p[0]), lanes(p[1])] for p in pairs}
    r_p = {p: rho[rows(p[0]), lanes(p[1])] for p in pairs}
    b_p = {p: beta[rows(p[0]), lanes(p[1])] for p in pairs}
    k_p = {p: kappa[rows(p[0]), lanes(p[1])] for p in pairs}
    v_p = {p: v16[rows(p[0]), lanes(p[1])] for p in pairs}
    ar_p = {p: jnp.concatenate([a_p[p], r_p[p]], axis=0) for p in pairs}

    arm = {}
    for (i, j, hh) in heads:
        m = lo if hh == 0 else jnp.logical_not(lo)
        arm[(i, j, hh)] = jnp.where(m, ar_p[(i, j)], 0.0).astype(BF16)
    mb = {h: _dot_nt(arm[h], b_p[h[:2]]) for h in heads}
    mk = {h: _dot_nt(arm[h], k_p[h[:2]]) for h in heads}
    mab = {h: jnp.where(strict, mb[h][:chunk], 0.0) for h in heads}
    mrb = {h: jnp.where(incl, mb[h][chunk:], 0.0).astype(BF16) for h in heads}
    mak = {h: jnp.where(strict, mk[h][:chunk], 0.0).astype(BF16) for h in heads}
    mrk = {h: jnp.where(incl, mk[h][chunk:], 0.0).astype(BF16) for h in heads}
    xh = {h: _dot(mak[h], v_p[h[:2]]) for h in heads}
    ov = {h: _dot(mrk[h], v_p[h[:2]]) for h in heads}
    pw = {h: (-mab[h]).astype(BF16) for h in heads}
    tinv = {h: eye_c - mab[h] for h in heads}
    for _ in range(n_sq):
        pw = {h: _dot(pw[h], pw[h]).astype(BF16) for h in heads}
        tinv = {h: tinv[h] + _dot(tinv[h].astype(BF16), pw[h]) for h in heads}
    wu = {h: _dot(tinv[h].astype(BF16), jnp.concatenate([a_p[h[:2]], xh[h]], axis=1).astype(BF16))
          for h in heads}
    y2 = {h: _dot(mrb[h], wu[h].astype(BF16)) for h in heads}

    def merge(f):
        return {p: jnp.where(lo, f((p[0], p[1], 0)), f((p[0], p[1], 1))) for p in pairs}

    w2 = merge(lambda h: wu[h][:, :LANES])
    u0 = merge(lambda h: wu[h][:, LANES:])
    rw = merge(lambda h: r_p[h[:2]] - y2[h][:, :LANES])
    o0 = merge(lambda h: ov[h] - y2[h][:, LANES:])

    s_old = {p: s_scr[p[0], p[1]] for p in pairs}
    s16 = {p: s_old[p].astype(BF16) for p in pairs}
    out = {p: _dot_nt(rw[p].astype(BF16), s16[p]) + o0[p] for p in pairs}
    w16 = {p: w2[p].astype(BF16) for p in pairs}
    be_p = {p: beta_e[rows(p[0]), lanes(p[1])] for p in pairs}
    ke_p = {p: kappa_e[rows(p[0]), lanes(p[1])] for p in pairs}
    wtb = {p: _dot_tn(w16[p], be_p[p]) for p in pairs}
    gm = {p: _dot_tn(jnp.concatenate([v_p[p], (-u0[p]).astype(BF16)], axis=0),
                     jnp.concatenate([ke_p[p], be_p[p]], axis=0)) for p in pairs}
    for p in pairs:
        i, j = p
        p_end = jnp.exp(cl[(i + 1) * chunk - 1:(i + 1) * chunk, lanes(j)])
        phi = jnp.where(bd_mask, jnp.where(eye, p_end, 0.0) - wtb[p], 0.0)
        s_scr[i, j] = _dot(s16[p], phi.astype(BF16)) + jnp.where(bd_mask, gm[p], 0.0)
    o = jnp.concatenate(
        [jnp.concatenate([out[(i, j)] for j in range(HEAD_PAIRS)], axis=1) for i in range(nb)], axis=0)

    mean = _seg_sum64(o, bd16) * (1.0 / NA)
    oc = o - mean
    var = _seg_sum64(oc * oc, bd16) * (1.0 / NA)
    o = oc * lax.rsqrt(var + 64e-5) * lnw_ref[...] + lnb_ref[...]
    o = o + _seg_sum64(r * k2 * rk_ref[...], bd16) * v
    y_ref[...] = (o * g).reshape(nb, chunk, WA).astype(y_ref.dtype)

    @pl.when(c == pl.num_programs(1) - 1)
    def _():
        sout_ref[...] = s_scr[...]


def _n_squarings(chunk):
    n, reach = 0, 1
    while reach < chunk - 1:
        n += 1
        reach = 2 * reach + 1
    return n


def _pair_states(s):
    b = s.shape[0]
    s = s.reshape(b, HEAD_PAIRS, 2, NA, NA)
    z = jnp.zeros((b, HEAD_PAIRS, NA, NA), s.dtype)
    top = jnp.concatenate([s[:, :, 0], z], axis=-1)
    bot = jnp.concatenate([z, s[:, :, 1]], axis=-1)
    return jnp.concatenate([top, bot], axis=-2)


def _unpair_states(sp):
    b = sp.shape[0]
    s = jnp.stack([sp[:, :, :NA, :NA], sp[:, :, NA:, NA:]], axis=2)
    return s.reshape(b, HA, NA, NA)


def _rwkv(pa3, shift0, s0, params, layer, chunk, nb):
    b, t, _ = pa3.shape
    nc = t // chunk
    vec = lambda n: pl.BlockSpec((None, 1, n), lambda i, c: (layer, 0, 0))
    mat = lambda r, n: pl.BlockSpec((None, r, n), lambda i, c: (layer, 0, 0))
    y, s_out = pl.pallas_call(
        functools.partial(_rwkv_body, nb=nb, chunk=chunk, n_sq=_n_squarings(chunk)),
        grid=(b // nb, nc),
        in_specs=[
            pl.BlockSpec((nb, chunk, PA), lambda i, c: (i, c, 0)),
            pl.BlockSpec((nb, 1, PA), lambda i, c: (i, 0, 0)),
            pl.BlockSpec((nb, HEAD_PAIRS, LANES, LANES), lambda i, c: (i, 0, 0, 0)),
            vec(PA), vec(WA), mat(LANES, WA), vec(WA), mat(LORA_G, WA), vec(WA), vec(WA), vec(WA), vec(WA), vec(WA),
        ],
        out_specs=[
            pl.BlockSpec((nb, chunk, WA), lambda i, c: (i, c, 0)),
            pl.BlockSpec((nb, HEAD_PAIRS, LANES, LANES), lambda i, c: (i, 0, 0, 0)),
        ],
        out_shape=[
            jax.ShapeDtypeStruct((b, t, WA), BF16),
            jax.ShapeDtypeStruct((b, HEAD_PAIRS, LANES, LANES), F32),
        ],
        scratch_shapes=[pltpu.VMEM((nb, 1, PA), F32), pltpu.VMEM((nb, HEAD_PAIRS, LANES, LANES), F32)],
        compiler_params=pltpu.CompilerParams(
            dimension_semantics=("parallel", "arbitrary"), vmem_limit_bytes=VMEM_LIMIT),
        name="rwkv7",
    )(pa3, shift0.reshape(b, 1, PA), _pair_states(s0), *params)
    return y, _unpair_states(s_out)


def _hgrn_body(q_ref, f_ref, i_ref, g_ref, s0_ref, lb_ref, nw_ref, y_ref, sout_ref, s_scr, *, nb, chunk):
    c = pl.program_id(1)

    @pl.when(c == 0)
    def _():
        s_scr[...] = s0_ref[...]

    lb = lb_ref[...]
    ti = lax.broadcasted_iota(jnp.int32, (chunk, chunk), 0)
    si = lax.broadcasted_iota(jnp.int32, (chunk, chunk), 1)
    incl = si <= ti
    tri = jnp.where(incl, 1.0, 0.0).astype(BF16)
    mid = (chunk - 1) // 2
    sl = lambda h: slice(h * LANES, (h + 1) * LANES)

    qa, ka, qe, kl, iv, last = {}, {}, {}, {}, {}, {}
    for i in range(nb):
        q = q_ref[i]
        fl = f_ref[i]
        f = lb + (1.0 - lb) * _sigmoid(fl)
        log_f = jnp.log(jnp.maximum(f, F_FLOOR))
        key = (1.0 - lb) * _sigmoid(-fl)
        cum = _dot_exact_lhs(tri, log_f, 3)
        anchor = cum[mid:mid + 1, :]
        last[i] = cum[chunk - 1:chunk, :]
        qa[i] = (q * jnp.exp(cum - anchor)).astype(BF16)
        ka[i] = (key * jnp.exp(anchor - cum)).astype(BF16)
        qe[i] = (q * jnp.exp(cum)).astype(BF16)
        kl[i] = (key * jnp.exp(last[i] - cum)).astype(BF16)
        iv[i] = i_ref[i].astype(BF16)

    chains = [(i, h) for i in range(nb) for h in range(HC)]
    s_old = {ch: s_scr[ch[0], ch[1]] for ch in chains}
    att = {(i, h): jnp.where(incl, _dot_nt(qa[i][:, sl(h)], ka[i][:, sl(h)]), 0.0).astype(BF16) for (i, h) in chains}
    inter = {(i, h): _dot(qe[i][:, sl(h)], s_old[(i, h)].astype(BF16)) for (i, h) in chains}
    upd = {(i, h): _dot_tn(kl[i][:, sl(h)], iv[i][:, sl(h)]) for (i, h) in chains}
    intra = {(i, h): _dot(att[(i, h)], iv[i][:, sl(h)]) for (i, h) in chains}
    for i in range(nb):
        outs = []
        for h in range(HC):
            o = intra[(i, h)] + inter[(i, h)]
            decay = jnp.exp(jnp.broadcast_to(last[i][:, sl(h)], (DKC, LANES)).T)
            s_scr[i, h] = decay * s_old[(i, h)] + upd[(i, h)]
            ms = jnp.mean(o * o, axis=-1, keepdims=True)
            outs.append(o * lax.rsqrt(ms + 1e-6) * nw_ref[...])
        g = g_ref[i]
        y_ref[i] = (jnp.concatenate(outs, axis=1) * (g * _sigmoid(g))).astype(y_ref.dtype)

    @pl.when(c == pl.num_programs(1) - 1)
    def _():
        sout_ref[...] = s_scr[...]


def _hgrn(pc3, s0, lb, norm_w, layer, chunk, nb):
    b, t, _ = pc3.shape
    nc = t // chunk
    col = lambda n: pl.BlockSpec((nb, chunk, WC), lambda i, c: (i, c, n))
    y, s_out = pl.pallas_call(
        functools.partial(_hgrn_body, nb=nb, chunk=chunk),
        grid=(b // nb, nc),
        in_specs=[
            col(0), col(1), col(2), col(3),
            pl.BlockSpec((nb, HC, DKC, DVC), lambda i, c: (i, 0, 0, 0)),
            pl.BlockSpec((None, 1, WC), lambda i, c: (layer, 0, 0)),
            pl.BlockSpec((None, 1, DVC), lambda i, c: (layer, 0, 0)),
        ],
        out_specs=[
            pl.BlockSpec((nb, chunk, WC), lambda i, c: (i, c, 0)),
            pl.BlockSpec((nb, HC, DKC, DVC), lambda i, c: (i, 0, 0, 0)),
        ],
        out_shape=[
            jax.ShapeDtypeStruct((b, t, WC), BF16),
            jax.ShapeDtypeStruct((b, HC, DKC, DVC), F32),
        ],
        scratch_shapes=[pltpu.VMEM((nb, HC, DKC, DVC), F32)],
        compiler_params=pltpu.CompilerParams(
            dimension_semantics=("parallel", "arbitrary"), vmem_limit_bytes=VMEM_LIMIT),
        name="hgrn2",
    )(pc3, pc3, pc3, pc3, s0, lb.reshape(DEPTH, 1, WC), norm_w.reshape(DEPTH, 1, DVC))
    return y, s_out


def _split_maps(q, scale):
    lane = lax.broadcasted_iota(jnp.int32, (1, LANES), 1)
    q = q * scale
    return jnp.concatenate([jnp.where(lane < DKB, q, 0.0), jnp.where(lane < DKB, 0.0, q)], axis=0).astype(BF16)


def _subln(o, w, scale):
    ms = jnp.mean(o * o, axis=-1, keepdims=True)
    return o * lax.rsqrt(ms + 1e-5) * w * scale


def _attn_prompt_body(lam_ref, q_ref, k_ref, v_ref, w_ref, o_ref, *, tq, out_scale):
    qi = pl.program_id(2)
    lam = lam_ref[0]
    qq = _split_maps(q_ref[0], DKB ** -0.5)
    qpos = qi * tq + lax.broadcasted_iota(jnp.int32, (tq, 1), 0)
    qpos = jnp.concatenate([qpos, qpos], axis=0)

    def key_block(kb, carry):
        m, l, acc = carry
        start = pl.multiple_of(kb * tq, 8)
        k = k_ref[0, pl.ds(start, tq), :].astype(BF16)
        v = v_ref[0, pl.ds(start, tq), :].astype(BF16)
        s = _dot_nt(qq, k)
        kpos = kb * tq + lax.broadcasted_iota(jnp.int32, (1, tq), 1)
        s = jnp.where(kpos <= qpos, s, MASK_VALUE)
        m_new = jnp.maximum(m, jnp.max(s, axis=-1, keepdims=True))
        e = jnp.exp(s - m_new).astype(BF16)
        corr = jnp.exp(m - m_new)
        l = corr * l + jnp.sum(e.astype(F32), axis=-1, keepdims=True)
        acc = corr * acc + _dot(e, v)
        return m_new, l, acc

    init = (jnp.full((2 * tq, 1), MASK_VALUE, F32), jnp.zeros((2 * tq, 1), F32), jnp.zeros((2 * tq, DVB), F32))
    _, l, acc = lax.fori_loop(0, qi + 1, key_block, init)
    o = acc * (1.0 / l)
    o = o[:tq] - lam * o[tq:]
    o_ref[0] = _subln(o, w_ref[...], out_scale).astype(o_ref.dtype)


def _attn_prompt(pb3, lam, subln_w, layer, out_scale):
    b, t, _ = pb3.shape
    tq = 344 if t % 344 == 0 else t
    return pl.pallas_call(
        functools.partial(_attn_prompt_body, tq=tq, out_scale=out_scale),
        grid=(b, HB, t // tq),
        in_specs=[
            pl.BlockSpec(memory_space=pltpu.SMEM),
            pl.BlockSpec((1, tq, LANES), lambda i, h, q: (i, q, h)),
            pl.BlockSpec((1, t, LANES), lambda i, h, q: (i, 0, HB + h)),
            pl.BlockSpec((1, t, LANES), lambda i, h, q: (i, 0, 2 * HB + h)),
            pl.BlockSpec((None, 1, DVB), lambda i, h, q: (layer, 0, 0)),
        ],
        out_specs=pl.BlockSpec((1, tq, LANES), lambda i, h, q: (i, q, h)),
        out_shape=jax.ShapeDtypeStruct((b, t, WB), BF16),
        compiler_params=pltpu.CompilerParams(
            dimension_semantics=("parallel", "parallel", "arbitrary"), vmem_limit_bytes=VMEM_LIMIT),
        name="diff_attn_prompt",
    )(lam, pb3, pb3, pb3, subln_w.reshape(DEPTH, 1, DVB))


def _attn_sample_body(pt_ref, lam_ref, *refs, n_pg, t, out_scale):
    del pt_ref
    q_ref, kn_ref, vn_ref, w_ref = refs[:4]
    k_refs = refs[4:4 + n_pg]
    v_refs = refs[4 + n_pg:4 + 2 * n_pg]
    o_ref, qq_scr, m_scr, l_scr, acc_scr = refs[4 + 2 * n_pg:]
    g = pl.program_id(1)
    hr = 2 * t
    page_rows = PAGE_SIZE * HB

    @pl.when(g == 0)
    def _():
        q = q_ref[0]
        qpos = lax.broadcasted_iota(jnp.int32, (t, 1), 0)
        qpos = jnp.concatenate([qpos, qpos], axis=0)
        kpos = lax.broadcasted_iota(jnp.int32, (1, t), 1)
        for h in range(HB):
            sl = slice(h * LANES, (h + 1) * LANES)
            rs = slice(h * hr, (h + 1) * hr)
            qq = _split_maps(q[:, sl], DKB ** -0.5)
            qq_scr[rs, :] = qq
            s = _dot_nt(qq, kn_ref[0][:, sl].astype(BF16))
            s = jnp.where(kpos <= qpos, s, MASK_VALUE)
            m = jnp.max(s, axis=-1, keepdims=True)
            e = jnp.exp(s - m)
            m_scr[rs, :] = m
            l_scr[rs, :] = jnp.sum(e, axis=-1, keepdims=True)
            acc_scr[rs, :] = _dot(e.astype(BF16), vn_ref[0][:, sl].astype(BF16))

    qq = qq_scr[...]
    rid = lax.broadcasted_iota(jnp.int32, (HB * hr, 1), 0)
    row_head = sum((rid >= h * hr).astype(jnp.int32) for h in range(1, HB))
    col_head = lax.broadcasted_iota(jnp.int32, (1, page_rows), 1) & (HB - 1)
    own = row_head == col_head
    s = jnp.concatenate(
        [jnp.where(own, _dot_nt(qq, kr[...].astype(BF16)), MASK_VALUE) for kr in k_refs], axis=1)
    m_old = m_scr[...]
    m_new = jnp.maximum(m_old, jnp.max(s, axis=-1, keepdims=True))
    e = jnp.exp(s - m_new).astype(BF16)
    corr = jnp.exp(m_old - m_new)
    pv = _dot(e[:, :page_rows], v_refs[0][...].astype(BF16))
    for n in range(1, n_pg):
        pv = pv + _dot(e[:, n * page_rows:(n + 1) * page_rows], v_refs[n][...].astype(BF16))
    l_scr[...] = corr * l_scr[...] + jnp.sum(e.astype(F32), axis=-1, keepdims=True)
    acc_scr[...] = corr * acc_scr[...] + pv
    m_scr[...] = m_new

    @pl.when(g == pl.num_programs(1) - 1)
    def _():
        lam = lam_ref[0]
        o = acc_scr[...] * (1.0 / l_scr[...])
        outs = []
        for h in range(HB):
            oh = o[h * hr:h * hr + t] - lam * o[h * hr + t:(h + 1) * hr]
            outs.append(_subln(oh, w_ref[...], out_scale))
        o_ref[0] = jnp.concatenate(outs, axis=1).astype(o_ref.dtype)


def _attn_sample(pb3, cache_k, cache_v, page_table, lam, subln_w, layer, out_scale):
    b, t, _ = pb3.shape
    n_pages = page_table.shape[1]
    n_pg = next(n for n in (16, 8, 4, 2, 1) if n_pages % n == 0)
    depth, n_pool = cache_k.shape[:2]
    assert HB & (HB - 1) == 0 and 2 * DKB == DVB == LANES
    ck = cache_k.reshape(depth, n_pool, PAGE_SIZE * HB, 2 * DKB)
    cv = cache_v.reshape(depth, n_pool, PAGE_SIZE * HB, DVB)

    def page_spec(n):
        return pl.BlockSpec((None, None, PAGE_SIZE * HB, LANES),
                            lambda i, g, pt: (layer, pt[i, g * n_pg + n], 0, 0))

    rows = HB * 2 * t
    grid_spec = pltpu.PrefetchScalarGridSpec(
        num_scalar_prefetch=1,
        grid=(b, n_pages // n_pg),
        in_specs=[
            pl.BlockSpec(memory_space=pltpu.SMEM),
            pl.BlockSpec((1, t, WB), lambda i, g, pt: (i, 0, 0)),
            pl.BlockSpec((1, t, WB), lambda i, g, pt: (i, 0, 1)),
            pl.BlockSpec((1, t, WB), lambda i, g, pt: (i, 0, 2)),
            pl.BlockSpec((None, 1, DVB), lambda i, g, pt: (layer, 0, 0)),
        ] + [page_spec(n) for n in range(n_pg)] * 2,
        out_specs=pl.BlockSpec((1, t, WB), lambda i, g, pt: (i, 0, 0)),
        scratch_shapes=[
            pltpu.VMEM((rows, LANES), BF16), pltpu.VMEM((rows, 1), F32), pltpu.VMEM((rows, 1), F32),
            pltpu.VMEM((rows, DVB), F32)],
    )
    return pl.pallas_call(
        functools.partial(_attn_sample_body, n_pg=n_pg, t=t, out_scale=out_scale),
        grid_spec=grid_spec,
        out_shape=jax.ShapeDtypeStruct((b, t, WB), BF16),
        compiler_params=pltpu.CompilerParams(
            dimension_semantics=("parallel", "arbitrary"), vmem_limit_bytes=VMEM_LIMIT),
        name="diff_attn_sample",
    )(page_table, lam, pb3, pb3, pb3, subln_w.reshape(DEPTH, 1, DVB), *([ck] * n_pg), *([cv] * n_pg))


def _trunk(h3, paged, rwkv0, shift0, hgrn0, conv0, wt, chunk, nb):
    b, t, _ = h3.shape
    m = b * t
    h = h3.reshape(m, D_MODEL)
    lbs = jax.nn.softmax(wt['hg_lb'].astype(F32), axis=0)
    lbs = jnp.cumsum(lbs, axis=0) - lbs[0]
    wa2 = jnp.concatenate([wt['rw_w2'], wt['rw_a2']], axis=1)
    row = lambda name, n: wt[name].reshape(DEPTH, 1, n)
    rw_params = (row('rw_mu', PA), row('rw_w0', WA), wa2, row('rw_a0', WA), wt['rw_g2'], row('rw_kk', WA),
                 row('rw_ka', WA), row('rw_rk', WA), row('rw_lnw', WA), row('rw_lnb', WA))
    ks, vs, sas, shs, scs, cvs = [], [], [], [], [], []
    for l in range(DEPTH):
        pa = _matmul(h, wt['w_in'], l, n_cols=PA, col_off=0, gain=wt['norm1'])
        pb = _matmul(h, wt['w_in'], l, n_cols=PB, col_off=PA // 512, gain=wt['norm1'])
        pc = _matmul(h, wt['w_in'], l, n_cols=PC, col_off=(PA + PB) // 512, gain=wt['norm1'])
        pa3 = pa.reshape(b, t, PA)
        pb3 = pb.reshape(b, t, PB)
        pc3 = pc.reshape(b, t, PC)
        ya, sa = _rwkv(pa3, shift0[l], rwkv0[l], rw_params, l, chunk, nb)
        lam_init = 0.8 - 0.6 * math.exp(-0.3 * l)
        lam = (jnp.exp(jnp.sum(wt['da_lq1'][l].astype(F32) * wt['da_lk1'][l].astype(F32)))
               - jnp.exp(jnp.sum(wt['da_lq2'][l].astype(F32) * wt['da_lk2'][l].astype(F32))) + lam_init)
        lam = lam.reshape(1).astype(F32)
        if paged is None:
            yb = _attn_prompt(pb3, lam, wt['da_subln'], l, 1.0 - lam_init)
        else:
            yb = _attn_sample(pb3, paged[0], paged[1], paged[2], lam, wt['da_subln'], l, 1.0 - lam_init)
        yc, sc = _hgrn(pc3, hgrn0[l], lbs, wt['hg_norm'], l, chunk, next(n for n in (4, 2, 1) if b % n == 0))
        mix = (ya.reshape(m, WA), yb.reshape(m, WB), yc.reshape(m, WC))
        h = _matmul(mix, wt['w_out'], l, n_cols=D_MODEL, residual=h)
        if t % 344 == 0:
            act, cb = _up_conv_gate(h, wt['norm2'], wt['ffn_up'], conv0, wt['ffn_conv'], wt['ffn_conv_b'], l, t)
        else:
            u3 = _matmul(h, wt['ffn_up'], l, n_cols=2 * D_FF, gain=wt['norm2']).reshape(b, t, 2 * D_FF)
            act = _conv_gate(u3, conv0, wt['ffn_conv'], wt['ffn_conv_b'], l).reshape(m, D_FF)
            cb = u3[:, t - (CONV_W - 1):, :]
        h = _matmul(act, wt['ffn_down'], l, n_cols=D_MODEL, tn=256, residual=h)
        ks.append(pb3[:, :, WB:2 * WB].reshape(b, t, HB, 2 * DKB))
        vs.append(pb3[:, :, 2 * WB:].reshape(b, t, HB, DVB))
        sas.append(sa)
        shs.append(pa3[:, t - 1, :])
        scs.append(sc)
        cvs.append(cb)
    y = _rmsnorm(h, wt['final_norm']).reshape(b, t, D_MODEL)
    return (y, jnp.stack(ks), jnp.stack(vs), jnp.stack(sas), jnp.stack(shs), jnp.stack(scs), jnp.stack(cvs))


def kernel(x_prompt, x_sample, cache_k, cache_v, state_rwkv, state_shift, state_hgrn, state_conv, page_table, meta, norm1, w_in, rw_mu, rw_w0, rw_w2, rw_a0, rw_a2, rw_g2, rw_kk, rw_ka, rw_rk, rw_lnw, rw_lnb, da_lq1, da_lk1, da_lq2, da_lk2, da_subln, hg_lb, hg_norm, w_out, norm2, ffn_up, ffn_conv, ffn_conv_b, ffn_down, final_norm):
    wt = {'norm1': norm1, 'w_in': w_in, 'rw_mu': rw_mu, 'rw_w0': rw_w0, 'rw_w2': rw_w2, 'rw_a0': rw_a0,
          'rw_a2': rw_a2, 'rw_g2': rw_g2, 'rw_kk': rw_kk, 'rw_ka': rw_ka, 'rw_rk': rw_rk, 'rw_lnw': rw_lnw,
          'rw_lnb': rw_lnb, 'da_lq1': da_lq1, 'da_lk1': da_lk1, 'da_lq2': da_lq2, 'da_lk2': da_lk2,
          'da_subln': da_subln, 'hg_lb': hg_lb, 'hg_norm': hg_norm, 'w_out': w_out, 'norm2': norm2,
          'ffn_up': ffn_up, 'ffn_conv': ffn_conv, 'ffn_conv_b': ffn_conv_b, 'ffn_down': ffn_down,
          'final_norm': final_norm}
    bp = x_prompt.shape[0]
    dt = x_prompt.dtype
    hp = jnp.concatenate([jnp.broadcast_to(meta[None].astype(dt), (bp, N_META, D_MODEL)), x_prompt], axis=1)
    tp = hp.shape[1]
    chunk_p = next(c for c in (48, 24, 16, 8) if tp % c == 0)
    yp, p_k, p_v, p_rwkv, p_shift, p_hgrn, p_conv = _trunk(
        hp, None,
        jnp.zeros((DEPTH, bp, HA, NA, NA), dt), jnp.zeros((DEPTH, bp, PA), dt),
        jnp.zeros((DEPTH, bp, HC, DKC, DVC), dt), jnp.zeros((DEPTH, bp, CONV_W - 1, 2 * D_FF), dt),
        wt, chunk_p, 2 if bp % 2 == 0 else 1)
    y_prompt = yp[:, N_META:]
    bs, ts = x_sample.shape[:2]
    y_sample, s_k, s_v, s_rwkv, s_shift, s_hgrn, s_conv = _trunk(
        x_sample, (cache_k, cache_v, page_table), state_rwkv, state_shift, state_hgrn, state_conv, wt, ts,
        4 if bs % 4 == 0 else 1)
    return (y_prompt, y_sample, p_k, p_v, p_rwkv, p_shift, p_hgrn, p_conv,
            s_k, s_v, s_rwkv, s_shift, s_hgrn, s_conv)
```

```python
import functools
import math

import jax
import jax.numpy as jnp
from jax import lax
from jax.experimental import pallas as pl
from jax.experimental.pallas import tpu as pltpu

F32 = jnp.float32
BF16 = jnp.bfloat16

D_MODEL = 2048
DEPTH = 2
N_META = 16
PAGE_SIZE = 128
WB = D_MODEL // 4
WA = (D_MODEL - WB) // 2
WC = D_MODEL - WA - WB
NA = 64
HA = WA // NA
LORA_W = 64
LORA_A = 64
LORA_G = 128
DVB = 128
DKB = DVB // 2
HB = WB // DVB
MASK_VALUE = -1e30
DVC = 128
DKC = 128
HC = WC // DVC
F_FLOOR = 1e-30
D_FF = ((8 * D_MODEL // 3 + 255) // 256) * 256
CONV_W = 3
PA = 3 * WA + LORA_W + LORA_A + LORA_G
PB = 2 * HB * 2 * DKB + HB * DVB
PC = 2 * HC * DKC + 2 * WC

LANES = 128
HEAD_PAIRS = HA // 2
HG_PIECES = 3
VMEM_LIMIT = 56 * 1024 * 1024


def _dot(a, b):
    return jnp.dot(a, b, preferred_element_type=F32)


def _dot_nt(a, b):
    return lax.dot_general(a, b, (((1,), (1,)), ((), ())), preferred_element_type=F32)


def _dot_tn(a, b):
    return lax.dot_general(a, b, (((0,), (0,)), ((), ())), preferred_element_type=F32)


def _bf16_pieces(x, n):
    pieces = []
    for _ in range(n - 1):
        p = x.astype(BF16)
        pieces.append(p)
        x = x - p.astype(F32)
    pieces.append(x.astype(BF16))
    return pieces


def _dot_exact_lhs(m16, x, n):
    ps = _bf16_pieces(x, n)
    acc = _dot(m16, ps[-1])
    for p in ps[-2::-1]:
        acc = acc + _dot(m16, p)
    return acc


def _dot_x3(a, b):
    ah, al = _bf16_pieces(a, 2)
    bh, bl = _bf16_pieces(b, 2)
    return _dot(ah, bh) + (_dot(al, bh) + _dot(ah, bl))


def _sigmoid(x):
    return 1.0 / (1.0 + jnp.exp(-x))


def _softplus(x):
    return jnp.maximum(x, 0.0) + jnp.log(1.0 + jnp.exp(-jnp.abs(x)))


def _mm_body(*refs, n_x, norm, eps, residual, sub):
    refs = list(refs)
    x_refs = [refs.pop(0) for _ in range(n_x)]
    g_ref = refs.pop(0) if norm else None
    w_ref = refs.pop(0)
    r_ref = refs.pop(0) if residual else None
    o_ref, xs_ref = refs

    @pl.when(pl.program_id(1) == 0)
    def _():
        tm = xs_ref.shape[0]
        col = 0
        for x_ref in x_refs:
            width = x_ref.shape[1]
            for s in range(0, tm, sub):
                x = x_ref[s:s + sub, :]
                if norm:
                    ms = jnp.mean(x * x, axis=-1, keepdims=True)
                    x = x * lax.rsqrt(ms + eps) * g_ref[...]
                xs_ref[s:s + sub, col:col + width] = x.astype(BF16)
            col += width

    acc = _dot(xs_ref[...], w_ref[...].astype(BF16))
    if residual:
        acc = acc + r_ref[...]
    o_ref[...] = acc


def _row_tile(m):
    for t in (1032, 688, 512, 344, 256, 128, 64, 32, 16, 8):
        if m % t == 0:
            return t
    raise ValueError(m)


def _matmul(x, w, layer, *, n_cols, col_off=0, tn=512, gain=None, eps=1e-6, residual=None):
    xs = x if isinstance(x, tuple) else (x,)
    m = xs[0].shape[0]
    k = sum(p.shape[1] for p in xs)
    tm = _row_tile(m)
    sub = 344 if tm % 344 == 0 else tm
    norm = gain is not None
    assert not norm or len(xs) == 1
    in_specs = [pl.BlockSpec((tm, p.shape[1]), lambda i, j: (i, 0)) for p in xs]
    args = list(xs)
    if norm:
        in_specs.append(pl.BlockSpec((None, 1, k), lambda i, j: (layer, 0, 0)))
        args.append(gain.reshape(gain.shape[0], 1, k))
    in_specs.append(pl.BlockSpec((None, k, tn), lambda i, j: (layer, 0, j + col_off)))
    args.append(w)
    if residual is not None:
        in_specs.append(pl.BlockSpec((tm, tn), lambda i, j: (i, j)))
        args.append(residual)
    return pl.pallas_call(
        functools.partial(_mm_body, n_x=len(xs), norm=norm, eps=eps, residual=residual is not None, sub=sub),
        grid=(m // tm, n_cols // tn),
        in_specs=in_specs,
        out_specs=pl.BlockSpec((tm, tn), lambda i, j: (i, j)),
        out_shape=jax.ShapeDtypeStruct((m, n_cols), F32),
        scratch_shapes=[pltpu.VMEM((tm, k), BF16)],
        compiler_params=pltpu.CompilerParams(
            dimension_semantics=("parallel", "arbitrary"), vmem_limit_bytes=VMEM_LIMIT),
        name="matmul",
    )(*args)


def _norm_body(x_ref, g_ref, o_ref, *, eps):
    x = x_ref[...]
    ms = jnp.mean(x * x, axis=-1, keepdims=True)
    o_ref[...] = x * lax.rsqrt(ms + eps) * g_ref[...]


def _rmsnorm(x3, gain, skip, eps=1e-6):
    if skip == 0:
        x3 = x3.reshape(1, -1, x3.shape[-1])
    b, t, k = x3.shape
    rows = t - skip
    tm = next(c for c in (256, 128, 64, 32, 16, 8) if rows % c == 0)
    return pl.pallas_call(
        functools.partial(_norm_body, eps=eps),
        grid=(b, rows // tm),
        in_specs=[pl.BlockSpec((pl.Element(1), pl.Element(tm), pl.Element(k)), lambda i, r: (i, pl.multiple_of(skip + r * tm, 8), 0)),
                  pl.BlockSpec((1, 1, k), lambda i, r: (0, 0, 0))],
        out_specs=pl.BlockSpec((1, tm, k), lambda i, r: (i, r, 0)),
        out_shape=jax.ShapeDtypeStruct((b, rows, k), F32),
        compiler_params=pltpu.CompilerParams(dimension_semantics=("parallel", "parallel")),
        name="final_norm",
    )(x3, gain.reshape(1, 1, k))


def _conv_body(ug_ref, uv_ref, bg_ref, bv_ref, wg_ref, wv_ref, cg_ref, cv_ref, o_ref):
    def conv(u_ref, buf_ref, w_ref, c_ref):
        u = u_ref[...]
        buf = buf_ref[...]
        w = w_ref[...]
        row = lax.broadcasted_iota(jnp.int32, (1, u.shape[1], 1), 1)
        b0 = buf[:, 0:1, :]
        b1 = buf[:, 1:2, :]
        u1 = jnp.where(row == 0, b1, pltpu.roll(u, 1, 1))
        u2 = jnp.where(row == 0, b0, jnp.where(row == 1, b1, pltpu.roll(u, 2, 1)))
        return c_ref[...] + u2 * w[0:1, :] + u1 * w[1:2, :] + u * w[2:3, :]

    gate = conv(ug_ref, bg_ref, wg_ref, cg_ref)
    val = conv(uv_ref, bv_ref, wv_ref, cv_ref)
    o_ref[...] = (gate * _sigmoid(gate) * val).astype(o_ref.dtype)


def _conv_gate(u3, buf, w_conv, b_conv, layer, *, tn=512):
    b, t, _ = u3.shape
    bb = b if t <= 64 else 1
    nj = D_FF // tn
    b_conv3 = b_conv.reshape(DEPTH, 1, 2 * D_FF)
    blk = lambda off: pl.BlockSpec((bb, t, tn), lambda i, j: (i, 0, j + off))
    bufs = lambda off: pl.BlockSpec((None, bb, CONV_W - 1, tn), lambda i, j: (layer, i, 0, j + off))
    ws = lambda off: pl.BlockSpec((None, CONV_W, tn), lambda i, j: (layer, 0, j + off))
    cs = lambda off: pl.BlockSpec((None, 1, tn), lambda i, j: (layer, 0, j + off))
    return pl.pallas_call(
        _conv_body,
        grid=(b // bb, nj),
        in_specs=[blk(0), blk(nj), bufs(0), bufs(nj), ws(0), ws(nj), cs(0), cs(nj)],
        out_specs=pl.BlockSpec((bb, t, tn), lambda i, j: (i, 0, j)),
        out_shape=jax.ShapeDtypeStruct((b, t, D_FF), BF16),
        compiler_params=pltpu.CompilerParams(
            dimension_semantics=("parallel", "parallel"), vmem_limit_bytes=VMEM_LIMIT),
        name="conv_gate",
    )(u3, u3, buf, buf, w_conv, w_conv, b_conv3, b_conv3)


def _up_conv_body(x_ref, g_ref, wg_ref, wv_ref, bg_ref, bv_ref, cwg_ref, cwv_ref, cbg_ref, cbv_ref,
                  act_ref, sg_ref, sv_ref, xs_ref, carry_ref, *, eps, sub, tiles_per_seq):
    i = pl.program_id(0)
    j = pl.program_id(1)
    tm = x_ref.shape[0]

    @pl.when((i == 0) & (j == 0))
    def _():
        carry_ref[...] = jnp.zeros_like(carry_ref)

    @pl.when(j == 0)
    def _():
        for s in range(0, tm, sub):
            x = x_ref[s:s + sub, :]
            ms = jnp.mean(x * x, axis=-1, keepdims=True)
            xs_ref[s:s + sub, :] = (x * lax.rsqrt(ms + eps) * g_ref[...]).astype(BF16)

    first = (i % tiles_per_seq) == 0
    row = lax.broadcasted_iota(jnp.int32, (tm, 1), 0)

    def conv(w_ref, buf_ref, cw_ref, cb_ref, s_ref, slot):
        u = _dot(xs_ref[...], w_ref[...].astype(BF16))
        before = jnp.where(first, buf_ref[0], carry_ref[j, slot])
        tail = u[tm - (CONV_W - 1):, :]
        carry_ref[j, slot] = tail
        s_ref[0] = tail
        cw = cw_ref[...]
        u1 = jnp.where(row == 0, before[1:2, :], pltpu.roll(u, 1, 0))
        u2 = jnp.where(row == 0, before[0:1, :], jnp.where(row == 1, before[1:2, :], pltpu.roll(u, 2, 0)))
        return cb_ref[...] + u2 * cw[0:1, :] + u1 * cw[1:2, :] + u * cw[2:3, :]

    gate = conv(wg_ref, bg_ref, cwg_ref, cbg_ref, sg_ref, 0)
    val = conv(wv_ref, bv_ref, cwv_ref, cbv_ref, sv_ref, 1)
    act_ref[...] = (gate * _sigmoid(gate) * val).astype(act_ref.dtype)


def _up_conv_gate(h, gain, w_up, buf, w_conv, b_conv, layer, t, *, tn=512, eps=1e-6):
    m, k = h.shape
    b = m // t
    tm = _row_tile(t)
    tiles_per_seq = t // tm
    nj = D_FF // tn
    b_conv3 = b_conv.reshape(DEPTH, 1, 2 * D_FF)
    wsp = lambda off: pl.BlockSpec((None, k, tn), lambda i, j: (layer, 0, j + off))
    bufs = lambda off: pl.BlockSpec((None, 1, CONV_W - 1, tn), lambda i, j: (layer, i // tiles_per_seq, 0, j + off))
    cws = lambda off: pl.BlockSpec((None, CONV_W, tn), lambda i, j: (layer, 0, j + off))
    cbs = lambda off: pl.BlockSpec((None, 1, tn), lambda i, j: (layer, 0, j + off))
    tail_spec = pl.BlockSpec((1, CONV_W - 1, tn), lambda i, j: (i, 0, j))
    act, sg, sv = pl.pallas_call(
        functools.partial(_up_conv_body, eps=eps, sub=344 if tm % 344 == 0 else tm, tiles_per_seq=tiles_per_seq),
        grid=(m // tm, nj),
        in_specs=[
            pl.BlockSpec((tm, k), lambda i, j: (i, 0)),
            pl.BlockSpec((None, 1, k), lambda i, j: (layer, 0, 0)),
            wsp(0), wsp(nj), bufs(0), bufs(nj), cws(0), cws(nj), cbs(0), cbs(nj),
        ],
        out_specs=[pl.BlockSpec((tm, tn), lambda i, j: (i, j)), tail_spec, tail_spec],
        out_shape=[
            jax.ShapeDtypeStruct((m, D_FF), BF16),
            jax.ShapeDtypeStruct((m // tm, CONV_W - 1, D_FF), F32),
            jax.ShapeDtypeStruct((m // tm, CONV_W - 1, D_FF), F32),
        ],
        scratch_shapes=[pltpu.VMEM((tm, k), BF16), pltpu.VMEM((nj, 2, CONV_W - 1, tn), F32)],
        compiler_params=pltpu.CompilerParams(
            dimension_semantics=("arbitrary", "arbitrary"), vmem_limit_bytes=VMEM_LIMIT),
        name="up_conv_gate",
    )(h, gain.reshape(DEPTH, 1, k), w_up, w_up, buf, buf, w_conv, w_conv, b_conv3, b_conv3)
    seq_end = slice(tiles_per_seq - 1, None, tiles_per_seq)
    return act, jnp.concatenate([sg[seq_end], sv[seq_end]], axis=-1)


def _seg_sum64(x, bd16):
    n = x.shape[0]
    st = jnp.concatenate(_bf16_pieces(x, 2), axis=0)
    outs = []
    for j in range(x.shape[1] // LANES):
        y = _dot(st[:, j * LANES:(j + 1) * LANES], bd16)
        outs.append(y[:n] + y[n:])
    return jnp.concatenate(outs, axis=1)


def _rwkv_body(x_ref, shift_ref, s0_ref, mu_ref, w0_ref, wa2_ref, a0_ref, g2_ref, kk_ref, ka_ref, rk_ref,
               lnw_ref, lnb_ref, y_ref, sout_ref, prev_scr, s_scr, *, nb, chunk, n_sq):
    c = pl.program_id(1)
    n = nb * chunk

    @pl.when(c == 0)
    def _():
        prev_scr[...] = shift_ref[...]
        s_scr[...] = s0_ref[...]

    x = x_ref[...].reshape(n, PA)
    row = lax.broadcasted_iota(jnp.int32, (n, 1), 0)
    prev = pltpu.roll(x, 1, 0)
    for i in range(nb):
        prev = jnp.where(row == i * chunk, prev_scr[i], prev)
        prev_scr[i] = x[(i + 1) * chunk - 1:(i + 1) * chunk, :]
    xm = x + (prev - x) * mu_ref[...]
    r = xm[:, :WA]
    k = xm[:, WA:2 * WA]
    v = xm[:, 2 * WA:3 * WA]
    wa_in = xm[:, 3 * WA:3 * WA + LANES]
    gd = xm[:, 3 * WA + LANES:]

    lane = lax.broadcasted_iota(jnp.int32, (1, LANES), 1)
    lo = lane < NA
    wa2 = wa2_ref[...]
    z = w0_ref[...] + _dot_x3(jnp.where(lo, jnp.tanh(wa_in), 0.0), wa2)
    w = -_softplus(-z) - 0.5
    logd = -jnp.exp(w)
    a = _sigmoid(a0_ref[...] + _dot(jnp.where(lo, 0.0, wa_in).astype(BF16), wa2.astype(BF16)))
    g = _dot(_sigmoid(gd).astype(BF16), g2_ref[...].astype(BF16))

    ri = lax.broadcasted_iota(jnp.int32, (LANES, LANES), 0)
    ci = lax.broadcasted_iota(jnp.int32, (LANES, LANES), 1)
    bd_mask = (ri < NA) == (ci < NA)
    bd16 = jnp.where(bd_mask, 1.0, 0.0).astype(BF16)
    eye = ri == ci

    kkr = k * kk_ref[...]
    kk = kkr / jnp.maximum(jnp.sqrt(_seg_sum64(kkr * kkr, bd16)), 1e-12)
    k2 = k * (1.0 + (a - 1.0) * ka_ref[...])
    b = kk * a

    tn_ = lax.broadcasted_iota(jnp.int32, (n, n), 0)
    sn_ = lax.broadcasted_iota(jnp.int32, (n, n), 1)
    same = None
    for i in range(nb):
        blk = ((tn_ >= i * chunk) & (tn_ < (i + 1) * chunk) & (sn_ >= i * chunk) & (sn_ < (i + 1) * chunk))
        same = blk if same is None else (same | blk)
    tri_incl = jnp.where(same & (sn_ <= tn_), 1.0, 0.0).astype(BF16)
    tri_rest = jnp.where(same & (sn_ > tn_), 1.0, 0.0).astype(BF16)
    cl = _dot_exact_lhs(tri_incl, logd, 3)
    rl = _dot_exact_lhs(tri_rest, logd, 3)
    e_neg = jnp.exp(-cl)
    e_end = jnp.exp(rl)
    alpha = kk * jnp.exp(cl - logd)
    rho = r * jnp.exp(cl)
    beta = (b * e_neg).astype(BF16)
    kappa = (k2 * e_neg).astype(BF16)
    beta_e = (b * e_end).astype(BF16)
    kappa_e = (k2 * e_end).astype(BF16)
    v16 = v.astype(BF16)

    ti = lax.broadcasted_iota(jnp.int32, (chunk, chunk), 0)
    si = lax.broadcasted_iota(jnp.int32, (chunk, chunk), 1)
    incl = si <= ti
    strict = si < ti
    eye_c = jnp.where(ti == si, 1.0, 0.0).astype(F32)

    rows = lambda i: slice(i * chunk, (i + 1) * chunk)
    lanes = lambda j: slice(j * LANES, (j + 1) * LANES)
    pairs = [(i, j) for i in range(nb) for j in range(HEAD_PAIRS)]
    heads = [(i, j, hh) for (i, j) in pairs for hh in (0, 1)]

    a_p = {p: alpha[rows(p[0]), lanes(p[1])] for p in pairs}
    r_p = {p: rho[rows(p[0]), lanes(p[1])] for p in pairs}
    b_p = {p: beta[rows(p[0]), lanes(p[1])] for p in pairs}
    k_p = {p: kappa[rows(p[0]), lanes(p[1])] for p in pairs}
    v_p = {p: v16[rows(p[0]), lanes(p[1])] for p in pairs}
    ar_p = {p: jnp.concatenate([a_p[p], r_p[p]], axis=0) for p in pairs}

    arm = {}
    for (i, j, hh) in heads:
        m = lo if hh == 0 else jnp.logical_not(lo)
        arm[(i, j, hh)] = jnp.where(m, ar_p[(i, j)], 0.0).astype(BF16)
    mb = {h: _dot_nt(arm[h], b_p[h[:2]]) for h in heads}
    mk = {h: _dot_nt(arm[h], k_p[h[:2]]) for h in heads}
    mab = {h: jnp.where(strict, mb[h][:chunk], 0.0) for h in heads}
    mrb = {h: jnp.where(incl, mb[h][chunk:], 0.0).astype(BF16) for h in heads}
    mak = {h: jnp.where(strict, mk[h][:chunk], 0.0).astype(BF16) for h in heads}
    mrk = {h: jnp.where(incl, mk[h][chunk:], 0.0).astype(BF16) for h in heads}
    xh = {h: _dot(mak[h], v_p[h[:2]]) for h in heads}
    ov = {h: _dot(mrk[h], v_p[h[:2]]) for h in heads}
    pw = {h: (-mab[h]).astype(BF16) for h in heads}
    tinv = {h: eye_c - mab[h] for h in heads}
    for _ in range(n_sq):
        pw = {h: _dot(pw[h], pw[h]).astype(BF16) for h in heads}
        tinv = {h: tinv[h] + _dot(tinv[h].astype(BF16), pw[h]) for h in heads}
    wu = {h: _dot(tinv[h].astype(BF16), jnp.concatenate([a_p[h[:2]], xh[h]], axis=1).astype(BF16))
          for h in heads}
    y2 = {h: _dot(mrb[h], wu[h].astype(BF16)) for h in heads}

    def merge(f):
        return {p: jnp.where(lo, f((p[0], p[1], 0)), f((p[0], p[1], 1))) for p in pairs}

    w2 = merge(lambda h: wu[h][:, :LANES])
    u0 = merge(lambda h: wu[h][:, LANES:])
    rw = merge(lambda h: r_p[h[:2]] - y2[h][:, :LANES])
    o0 = merge(lambda h: ov[h] - y2[h][:, LANES:])

    s_old = {p: s_scr[p[0], p[1]] for p in pairs}
    s16 = {p: s_old[p].astype(BF16) for p in pairs}
    out = {p: _dot_nt(rw[p].astype(BF16), s16[p]) + o0[p] for p in pairs}
    w16 = {p: w2[p].astype(BF16) for p in pairs}
    be_p = {p: beta_e[rows(p[0]), lanes(p[1])] for p in pairs}
    ke_p = {p: kappa_e[rows(p[0]), lanes(p[1])] for p in pairs}
    wtb = {p: _dot_tn(w16[p], be_p[p]) for p in pairs}
    gm = {p: _dot_tn(jnp.concatenate([v_p[p], (-u0[p]).astype(BF16)], axis=0),
                     jnp.concatenate([ke_p[p], be_p[p]], axis=0)) for p in pairs}
    for p in pairs:
        i, j = p
        p_end = jnp.exp(cl[(i + 1) * chunk - 1:(i + 1) * chunk, lanes(j)])
        phi = jnp.where(bd_mask, jnp.where(eye, p_end, 0.0) - wtb[p], 0.0)
        s_scr[i, j] = _dot(s16[p], phi.astype(BF16)) + jnp.where(bd_mask, gm[p], 0.0)
    o = jnp.concatenate(
        [jnp.concatenate([out[(i, j)] for j in range(HEAD_PAIRS)], axis=1) for i in range(nb)], axis=0)

    mean = _seg_sum64(o, bd16) * (1.0 / NA)
    oc = o - mean
    var = _seg_sum64(oc * oc, bd16) * (1.0 / NA)
    o = oc * lax.rsqrt(var + 64e-5) * lnw_ref[...] + lnb_ref[...]
    o = o + _seg_sum64(r * k2 * rk_ref[...], bd16) * v
    y_ref[...] = (o * g).reshape(nb, chunk, WA).astype(y_ref.dtype)

    @pl.when(c == pl.num_programs(1) - 1)
    def _():
        sout_ref[...] = s_scr[...]


def _n_squarings(chunk):
    n, reach = 0, 1
    while reach < chunk - 1:
        n += 1
        reach = 2 * reach + 1
    return n


def _pair_states(s):
    b = s.shape[0]
    s = s.reshape(b, HEAD_PAIRS, 2, NA, NA)
    z = jnp.zeros((b, HEAD_PAIRS, NA, NA), s.dtype)
    top = jnp.concatenate([s[:, :, 0], z], axis=-1)
    bot = jnp.concatenate([z, s[:, :, 1]], axis=-1)
    return jnp.concatenate([top, bot], axis=-2)


def _unpair_states(sp):
    b = sp.shape[0]
    s = jnp.stack([sp[:, :, :NA, :NA], sp[:, :, NA:, NA:]], axis=2)
    return s.reshape(b, HA, NA, NA)


def _rwkv(pa3, shift0, s0, params, layer, chunk, nb):
    b, t, _ = pa3.shape
    nc = t // chunk
    vec = lambda n: pl.BlockSpec((None, 1, n), lambda i, c: (layer, 0, 0))
    mat = lambda r, n: pl.BlockSpec((None, r, n), lambda i, c: (layer, 0, 0))
    y, s_out = pl.pallas_call(
        functools.partial(_rwkv_body, nb=nb, chunk=chunk, n_sq=_n_squarings(chunk)),
        grid=(b // nb, nc),
        in_specs=[
            pl.BlockSpec((nb, chunk, PA), lambda i, c: (i, c, 0)),
            pl.BlockSpec((nb, 1, PA), lambda i, c: (i, 0, 0)),
            pl.BlockSpec((nb, HEAD_PAIRS, LANES, LANES), lambda i, c: (i, 0, 0, 0)),
            vec(PA), vec(WA), mat(LANES, WA), vec(WA), mat(LORA_G, WA), vec(WA), vec(WA), vec(WA), vec(WA), vec(WA),
        ],
        out_specs=[
            pl.BlockSpec((nb, chunk, WA), lambda i, c: (i, c, 0)),
            pl.BlockSpec((nb, HEAD_PAIRS, LANES, LANES), lambda i, c: (i, 0, 0, 0)),
        ],
        out_shape=[
            jax.ShapeDtypeStruct((b, t, WA), BF16),
            jax.ShapeDtypeStruct((b, HEAD_PAIRS, LANES, LANES), F32),
        ],
        scratch_shapes=[pltpu.VMEM((nb, 1, PA), F32), pltpu.VMEM((nb, HEAD_PAIRS, LANES, LANES), F32)],
        compiler_params=pltpu.CompilerParams(
            dimension_semantics=("parallel", "arbitrary"), vmem_limit_bytes=VMEM_LIMIT),
        name="rwkv7",
    )(pa3, shift0.reshape(b, 1, PA), _pair_states(s0), *params)
    return y, _unpair_states(s_out)


def _hgrn_body(*refs, nb, chunk):
    pieces = refs[:4 * HG_PIECES]
    s0_ref, lb_ref, nw_ref, y_ref, sout_ref, s_scr = refs[4 * HG_PIECES:]
    group = lambda n, i: jnp.concatenate([r[i] for r in pieces[n * HG_PIECES:(n + 1) * HG_PIECES]], axis=1)
    c = pl.program_id(1)

    @pl.when(c == 0)
    def _():
        s_scr[...] = s0_ref[...]

    lb = lb_ref[...]
    ti = lax.broadcasted_iota(jnp.int32, (chunk, chunk), 0)
    si = lax.broadcasted_iota(jnp.int32, (chunk, chunk), 1)
    incl = si <= ti
    tri = jnp.where(incl, 1.0, 0.0).astype(BF16)
    mid = (chunk - 1) // 2
    sl = lambda h: slice(h * LANES, (h + 1) * LANES)

    qa, ka, qe, kl, iv, last = {}, {}, {}, {}, {}, {}
    for i in range(nb):
        q = group(0, i)
        fl = group(1, i)
        f = lb + (1.0 - lb) * _sigmoid(fl)
        log_f = jnp.log(jnp.maximum(f, F_FLOOR))
        key = (1.0 - lb) * _sigmoid(-fl)
        cum = _dot_exact_lhs(tri, log_f, 3)
        anchor = cum[mid:mid + 1, :]
        last[i] = cum[chunk - 1:chunk, :]
        qa[i] = (q * jnp.exp(cum - anchor)).astype(BF16)
        ka[i] = (key * jnp.exp(anchor - cum)).astype(BF16)
        qe[i] = (q * jnp.exp(cum)).astype(BF16)
        kl[i] = (key * jnp.exp(last[i] - cum)).astype(BF16)
        iv[i] = group(2, i).astype(BF16)

    chains = [(i, h) for i in range(nb) for h in range(HC)]
    s_old = {ch: s_scr[ch[0], ch[1]] for ch in chains}
    att = {(i, h): jnp.where(incl, _dot_nt(qa[i][:, sl(h)], ka[i][:, sl(h)]), 0.0).astype(BF16) for (i, h) in chains}
    inter = {(i, h): _dot(qe[i][:, sl(h)], s_old[(i, h)].astype(BF16)) for (i, h) in chains}
    upd = {(i, h): _dot_tn(kl[i][:, sl(h)], iv[i][:, sl(h)]) for (i, h) in chains}
    intra = {(i, h): _dot(att[(i, h)], iv[i][:, sl(h)]) for (i, h) in chains}
    for i in range(nb):
        outs = []
        for h in range(HC):
            o = intra[(i, h)] + inter[(i, h)]
            decay = jnp.exp(jnp.broadcast_to(last[i][:, sl(h)], (DKC, LANES)).T)
            s_scr[i, h] = decay * s_old[(i, h)] + upd[(i, h)]
            ms = jnp.mean(o * o, axis=-1, keepdims=True)
            outs.append(o * lax.rsqrt(ms + 1e-6) * nw_ref[...])
        g = group(3, i)
        y_ref[i] = (jnp.concatenate(outs, axis=1) * (g * _sigmoid(g))).astype(y_ref.dtype)

    @pl.when(c == pl.num_programs(1) - 1)
    def _():
        sout_ref[...] = s_scr[...]


def _hgrn(p3, s0, lb, norm_w, layer, chunk, nb):
    b, t, _ = p3.shape
    nc = t // chunk
    width = WC // HG_PIECES
    c0 = (PA + PB) // width
    col = lambda n: pl.BlockSpec((nb, chunk, width), lambda i, c: (i, c, c0 + n))
    y, s_out = pl.pallas_call(
        functools.partial(_hgrn_body, nb=nb, chunk=chunk),
        grid=(b // nb, nc),
        in_specs=[
            *[col(n) for n in range(4 * HG_PIECES)],
            pl.BlockSpec((nb, HC, DKC, DVC), lambda i, c: (i, 0, 0, 0)),
            pl.BlockSpec((None, 1, WC), lambda i, c: (layer, 0, 0)),
            pl.BlockSpec((None, 1, DVC), lambda i, c: (layer, 0, 0)),
        ],
        out_specs=[
            pl.BlockSpec((nb, chunk, WC), lambda i, c: (i, c, 0)),
            pl.BlockSpec((nb, HC, DKC, DVC), lambda i, c: (i, 0, 0, 0)),
        ],
        out_shape=[
            jax.ShapeDtypeStruct((b, t, WC), BF16),
            jax.ShapeDtypeStruct((b, HC, DKC, DVC), F32),
        ],
        scratch_shapes=[pltpu.VMEM((nb, HC, DKC, DVC), F32)],
        compiler_params=pltpu.CompilerParams(
            dimension_semantics=("parallel", "arbitrary"), vmem_limit_bytes=VMEM_LIMIT),
        name="hgrn2",
    )(*([p3] * (4 * HG_PIECES)), s0, lb.reshape(DEPTH, 1, WC), norm_w.reshape(DEPTH, 1, DVC))
    return y, s_out


def _split_maps(q, scale):
    lane = lax.broadcasted_iota(jnp.int32, (1, LANES), 1)
    q = q * scale
    return jnp.concatenate([jnp.where(lane < DKB, q, 0.0), jnp.where(lane < DKB, 0.0, q)], axis=0).astype(BF16)


def _subln(o, w, scale):
    ms = jnp.mean(o * o, axis=-1, keepdims=True)
    return o * lax.rsqrt(ms + 1e-5) * w * scale


def _attn_prompt_body(lam_ref, q_ref, k_ref, v_ref, w_ref, *refs, tq, nq, out_scale, n_carried):
    o_ref, ko_ref, vo_ref = refs[n_carried:]
    h = pl.program_id(1)
    qi = pl.program_id(2)
    t = k_ref.shape[1]
    lam = lam_ref[0]

    @pl.when(qi == 0)
    def _():
        for hh in range(HB):
            @pl.when(h == hh)
            def _(hh=hh):
                ko_ref[0, pl.ds(hh, t, stride=HB), :] = k_ref[0]
                vo_ref[0, pl.ds(hh, t, stride=HB), :] = v_ref[0]

    qq = _split_maps(q_ref[0], DKB ** -0.5)
    for n in range(nq):
        @pl.when(qi == n)
        def _(n=n):
            ext = (n + 1) * tq
            s = _dot_nt(qq, k_ref[0, :ext, :].astype(BF16))
            qpos = n * tq + lax.broadcasted_iota(jnp.int32, (tq, 1), 0)
            qpos = jnp.concatenate([qpos, qpos], axis=0)
            kpos = lax.broadcasted_iota(jnp.int32, (1, ext), 1)
            s = jnp.where(kpos <= qpos, s, MASK_VALUE)
            e = jnp.exp(s - jnp.max(s, axis=-1, keepdims=True))
            p = e * (1.0 / jnp.sum(e, axis=-1, keepdims=True))
            att = p[:tq] - lam * p[tq:]
            o = _dot(att.astype(BF16), v_ref[0, :ext, :].astype(BF16))
            o_ref[0] = _subln(o, w_ref[...], out_scale).astype(o_ref.dtype)


def _attn_prompt(p3, lam, subln_w, layer, out_scale, kv_all=None):
    b, t, _ = p3.shape
    tq = 344 if t % 344 == 0 else t
    c0 = PA // LANES
    kv_spec = pl.BlockSpec((None, 1, t * HB, LANES), lambda i, h, q: (layer, i, 0, 0))
    carried = () if kv_all is None else tuple(kv_all)
    n_in = 5
    yb, ko, vo = pl.pallas_call(
        functools.partial(_attn_prompt_body, tq=tq, nq=t // tq, out_scale=out_scale, n_carried=len(carried)),
        grid=(b, HB, t // tq),
        in_specs=[
            pl.BlockSpec(memory_space=pltpu.SMEM),
            pl.BlockSpec((1, tq, LANES), lambda i, h, q: (i, q, c0 + h)),
            pl.BlockSpec((1, t, LANES), lambda i, h, q: (i, 0, c0 + HB + h)),
            pl.BlockSpec((1, t, LANES), lambda i, h, q: (i, 0, c0 + 2 * HB + h)),
            pl.BlockSpec((None, 1, DVB), lambda i, h, q: (layer, 0, 0)),
        ] + [pl.BlockSpec(memory_space=pl.ANY)] * len(carried),
        out_specs=[pl.BlockSpec((1, tq, LANES), lambda i, h, q: (i, q, h)), kv_spec, kv_spec],
        out_shape=[
            jax.ShapeDtypeStruct((b, t, WB), BF16),
            jax.ShapeDtypeStruct((DEPTH, b, t * HB, 2 * DKB), F32),
            jax.ShapeDtypeStruct((DEPTH, b, t * HB, DVB), F32),
        ],
        input_output_aliases={n_in + n: 1 + n for n in range(len(carried))},
        compiler_params=pltpu.CompilerParams(
            dimension_semantics=("parallel", "arbitrary", "arbitrary"), vmem_limit_bytes=VMEM_LIMIT),
        name="diff_attn_prompt",
    )(lam, p3, p3, p3, subln_w.reshape(DEPTH, 1, DVB), *carried)
    return yb, (ko, vo)


def _attn_sample_body(pt_ref, lam_ref, *refs, n_pg, t, out_scale):
    del pt_ref
    q_ref, kn_ref, vn_ref, w_ref = refs[:4]
    k_refs = refs[4:4 + n_pg]
    v_refs = refs[4 + n_pg:4 + 2 * n_pg]
    o_ref, qq_scr, m_scr, l_scr, acc_scr = refs[4 + 2 * n_pg:]
    g = pl.program_id(1)
    hr = 2 * t
    page_rows = PAGE_SIZE * HB

    @pl.when(g == 0)
    def _():
        q = q_ref[0]
        qpos = lax.broadcasted_iota(jnp.int32, (t, 1), 0)
        qpos = jnp.concatenate([qpos, qpos], axis=0)
        kpos = lax.broadcasted_iota(jnp.int32, (1, t), 1)
        for h in range(HB):
            sl = slice(h * LANES, (h + 1) * LANES)
            rs = slice(h * hr, (h + 1) * hr)
            qq = _split_maps(q[:, sl], DKB ** -0.5)
            qq_scr[rs, :] = qq
            s = _dot_nt(qq, kn_ref[0][:, sl].astype(BF16))
            s = jnp.where(kpos <= qpos, s, MASK_VALUE)
            m = jnp.max(s, axis=-1, keepdims=True)
            e = jnp.exp(s - m)
            m_scr[rs, :] = m
            l_scr[rs, :] = jnp.sum(e, axis=-1, keepdims=True)
            acc_scr[rs, :] = _dot(e.astype(BF16), vn_ref[0][:, sl].astype(BF16))

    qq = qq_scr[...]
    rid = lax.broadcasted_iota(jnp.int32, (HB * hr, 1), 0)
    row_head = sum((rid >= h * hr).astype(jnp.int32) for h in range(1, HB))
    col_head = lax.broadcasted_iota(jnp.int32, (1, page_rows), 1) & (HB - 1)
    own = row_head == col_head
    s = jnp.concatenate(
        [jnp.where(own, _dot_nt(qq, kr[...].astype(BF16)), MASK_VALUE) for kr in k_refs], axis=1)
    m_old = m_scr[...]
    m_new = jnp.maximum(m_old, jnp.max(s, axis=-1, keepdims=True))
    e = jnp.exp(s - m_new).astype(BF16)
    corr = jnp.exp(m_old - m_new)
    pv = _dot(e[:, :page_rows], v_refs[0][...].astype(BF16))
    for n in range(1, n_pg):
        pv = pv + _dot(e[:, n * page_rows:(n + 1) * page_rows], v_refs[n][...].astype(BF16))
    l_scr[...] = corr * l_scr[...] + jnp.sum(e.astype(F32), axis=-1, keepdims=True)
    acc_scr[...] = corr * acc_scr[...] + pv
    m_scr[...] = m_new

    @pl.when(g == pl.num_programs(1) - 1)
    def _():
        lam = lam_ref[0]
        o = acc_scr[...] * (1.0 / l_scr[...])
        outs = []
        for h in range(HB):
            oh = o[h * hr:h * hr + t] - lam * o[h * hr + t:(h + 1) * hr]
            outs.append(_subln(oh, w_ref[...], out_scale))
        o_ref[0] = jnp.concatenate(outs, axis=1).astype(o_ref.dtype)


def _attn_sample(pb3, cache_k, cache_v, page_table, lam, subln_w, layer, out_scale):
    b, t, _ = pb3.shape
    n_pages = page_table.shape[1]
    n_pg = next(n for n in (16, 8, 4, 2, 1) if n_pages % n == 0)
    depth, n_pool = cache_k.shape[:2]
    assert HB & (HB - 1) == 0 and 2 * DKB == DVB == LANES
    ck = cache_k.reshape(depth, n_pool, PAGE_SIZE * HB, 2 * DKB)
    cv = cache_v.reshape(depth, n_pool, PAGE_SIZE * HB, DVB)

    def page_spec(n):
        return pl.BlockSpec((None, None, PAGE_SIZE * HB, LANES),
                            lambda i, g, pt: (layer, pt[i, g * n_pg + n], 0, 0))

    rows = HB * 2 * t
    c0 = PA // WB
    grid_spec = pltpu.PrefetchScalarGridSpec(
        num_scalar_prefetch=1,
        grid=(b, n_pages // n_pg),
        in_specs=[
            pl.BlockSpec(memory_space=pltpu.SMEM),
            pl.BlockSpec((1, t, WB), lambda i, g, pt: (i, 0, c0)),
            pl.BlockSpec((1, t, WB), lambda i, g, pt: (i, 0, c0 + 1)),
            pl.BlockSpec((1, t, WB), lambda i, g, pt: (i, 0, c0 + 2)),
            pl.BlockSpec((None, 1, DVB), lambda i, g, pt: (layer, 0, 0)),
        ] + [page_spec(n) for n in range(n_pg)] * 2,
        out_specs=pl.BlockSpec((1, t, WB), lambda i, g, pt: (i, 0, 0)),
        scratch_shapes=[
            pltpu.VMEM((rows, LANES), BF16), pltpu.VMEM((rows, 1), F32), pltpu.VMEM((rows, 1), F32),
            pltpu.VMEM((rows, DVB), F32)],
    )
    return pl.pallas_call(
        functools.partial(_attn_sample_body, n_pg=n_pg, t=t, out_scale=out_scale),
        grid_spec=grid_spec,
        out_shape=jax.ShapeDtypeStruct((b, t, WB), BF16),
        compiler_params=pltpu.CompilerParams(
            dimension_semantics=("parallel", "arbitrary"), vmem_limit_bytes=VMEM_LIMIT),
        name="diff_attn_sample",
    )(page_table, lam, pb3, pb3, pb3, subln_w.reshape(DEPTH, 1, DVB), *([ck] * n_pg), *([cv] * n_pg))


def _trunk(h3, paged, rwkv0, shift0, hgrn0, conv0, wt, chunk, nb, skip):
    b, t, _ = h3.shape
    m = b * t
    h = h3.reshape(m, D_MODEL)
    lbs = jax.nn.softmax(wt['hg_lb'].astype(F32), axis=0)
    lbs = jnp.cumsum(lbs, axis=0) - lbs[0]
    wa2 = jnp.concatenate([wt['rw_w2'], wt['rw_a2']], axis=1)
    row = lambda name, n: wt[name].reshape(DEPTH, 1, n)
    rw_params = (row('rw_mu', PA), row('rw_w0', WA), wa2, row('rw_a0', WA), wt['rw_g2'], row('rw_kk', WA),
                 row('rw_ka', WA), row('rw_rk', WA), row('rw_lnw', WA), row('rw_lnb', WA))
    ks, vs, sas, shs, scs, cvs = [], [], [], [], [], []
    kv_all = None
    for l in range(DEPTH):
        p3 = _matmul(h, wt['w_in'], l, n_cols=PA + PB + PC, gain=wt['norm1']).reshape(b, t, PA + PB + PC)
        ya, sa = _rwkv(p3, shift0[l], rwkv0[l], rw_params, l, chunk, nb)
        lam_init = 0.8 - 0.6 * math.exp(-0.3 * l)
        lam = (jnp.exp(jnp.sum(wt['da_lq1'][l].astype(F32) * wt['da_lk1'][l].astype(F32)))
               - jnp.exp(jnp.sum(wt['da_lq2'][l].astype(F32) * wt['da_lk2'][l].astype(F32))) + lam_init)
        lam = lam.reshape(1).astype(F32)
        if paged is None:
            yb, kv_all = _attn_prompt(p3, lam, wt['da_subln'], l, 1.0 - lam_init, kv_all)
        else:
            yb = _attn_sample(p3, paged[0], paged[1], paged[2], lam, wt['da_subln'], l, 1.0 - lam_init)
            ks.append(p3[:, :, PA + WB:PA + 2 * WB].reshape(b, t, HB, 2 * DKB))
            vs.append(p3[:, :, PA + 2 * WB:PA + 3 * WB].reshape(b, t, HB, DVB))
        yc, sc = _hgrn(p3, hgrn0[l], lbs, wt['hg_norm'], l, chunk, next(n for n in (4, 2, 1) if b % n == 0))
        mix = (ya.reshape(m, WA), yb.reshape(m, WB), yc.reshape(m, WC))
        h = _matmul(mix, wt['w_out'], l, n_cols=D_MODEL, residual=h)
        if t % 344 == 0:
            act, cb = _up_conv_gate(h, wt['norm2'], wt['ffn_up'], conv0, wt['ffn_conv'], wt['ffn_conv_b'], l, t)
        else:
            u3 = _matmul(h, wt['ffn_up'], l, n_cols=2 * D_FF, gain=wt['norm2']).reshape(b, t, 2 * D_FF)
            act = _conv_gate(u3, conv0, wt['ffn_conv'], wt['ffn_conv_b'], l).reshape(m, D_FF)
            cb = u3[:, t - (CONV_W - 1):, :]
        h = _matmul(act, wt['ffn_down'], l, n_cols=D_MODEL, tn=256, residual=h)
        sas.append(sa)
        shs.append(p3[:, t - 1, :PA])
        scs.append(sc)
        cvs.append(cb)
    y = _rmsnorm(h.reshape(b, t, D_MODEL), wt['final_norm'], skip).reshape(b, t - skip, D_MODEL)
    if paged is None:
        k_out = kv_all[0].reshape(DEPTH, b, t, HB, 2 * DKB)
        v_out = kv_all[1].reshape(DEPTH, b, t, HB, DVB)
    else:
        k_out, v_out = jnp.stack(ks), jnp.stack(vs)
    return (y, k_out, v_out, jnp.stack(sas), jnp.stack(shs), jnp.stack(scs), jnp.stack(cvs))


def kernel(x_prompt, x_sample, cache_k, cache_v, state_rwkv, state_shift, state_hgrn, state_conv, page_table, meta, norm1, w_in, rw_mu, rw_w0, rw_w2, rw_a0, rw_a2, rw_g2, rw_kk, rw_ka, rw_rk, rw_lnw, rw_lnb, da_lq1, da_lk1, da_lq2, da_lk2, da_subln, hg_lb, hg_norm, w_out, norm2, ffn_up, ffn_conv, ffn_conv_b, ffn_down, final_norm):
    wt = {'norm1': norm1, 'w_in': w_in, 'rw_mu': rw_mu, 'rw_w0': rw_w0, 'rw_w2': rw_w2, 'rw_a0': rw_a0,
          'rw_a2': rw_a2, 'rw_g2': rw_g2, 'rw_kk': rw_kk, 'rw_ka': rw_ka, 'rw_rk': rw_rk, 'rw_lnw': rw_lnw,
          'rw_lnb': rw_lnb, 'da_lq1': da_lq1, 'da_lk1': da_lk1, 'da_lq2': da_lq2, 'da_lk2': da_lk2,
          'da_subln': da_subln, 'hg_lb': hg_lb, 'hg_norm': hg_norm, 'w_out': w_out, 'norm2': norm2,
          'ffn_up': ffn_up, 'ffn_conv': ffn_conv, 'ffn_conv_b': ffn_conv_b, 'ffn_down': ffn_down,
          'final_norm': final_norm}
    bp = x_prompt.shape[0]
    dt = x_prompt.dtype
    hp = jnp.concatenate([jnp.broadcast_to(meta[None].astype(dt), (bp, N_META, D_MODEL)), x_prompt], axis=1)
    tp = hp.shape[1]
    chunk_p = next(c for c in (48, 24, 16, 8) if tp % c == 0)
    y_prompt, p_k, p_v, p_rwkv, p_shift, p_hgrn, p_conv = _trunk(
        hp, None,
        jnp.zeros((DEPTH, bp, HA, NA, NA), dt), jnp.zeros((DEPTH, bp, PA), dt),
        jnp.zeros((DEPTH, bp, HC, DKC, DVC), dt), jnp.zeros((DEPTH, bp, CONV_W - 1, 2 * D_FF), dt),
        wt, chunk_p, 2 if bp % 2 == 0 else 1, N_META)
    bs, ts = x_sample.shape[:2]
    y_sample, s_k, s_v, s_rwkv, s_shift, s_hgrn, s_conv = _trunk(
        x_sample, (cache_k, cache_v, page_table), state_rwkv, state_shift, state_hgrn, state_conv, wt, ts,
        4 if bs % 4 == 0 else 1, 0)
    return (y_prompt, y_sample, p_k, p_v, p_rwkv, p_shift, p_hgrn, p_conv,
            s_k, s_v, s_rwkv, s_shift, s_hgrn, s_conv)
```

```python
import functools
import math

import jax
import jax.numpy as jnp
from jax import lax
from jax.experimental import pallas as pl
from jax.experimental.pallas import tpu as pltpu

F32 = jnp.float32
BF16 = jnp.bfloat16

D_MODEL = 2048
DEPTH = 2
N_META = 16
PAGE_SIZE = 128
WB = D_MODEL // 4
WA = (D_MODEL - WB) // 2
WC = D_MODEL - WA - WB
NA = 64
HA = WA // NA
LORA_W = 64
LORA_A = 64
LORA_G = 128
DVB = 128
DKB = DVB // 2
HB = WB // DVB
MASK_VALUE = -1e30
DVC = 128
DKC = 128
HC = WC // DVC
F_FLOOR = 1e-30
D_FF = ((8 * D_MODEL // 3 + 255) // 256) * 256
CONV_W = 3
PA = 3 * WA + LORA_W + LORA_A + LORA_G
PB = 2 * HB * 2 * DKB + HB * DVB
PC = 2 * HC * DKC + 2 * WC

LANES = 128
HEAD_PAIRS = HA // 2
HG_PIECES = 3
VMEM_LIMIT = 56 * 1024 * 1024


def _dot(a, b):
    return jnp.dot(a, b, preferred_element_type=F32)


def _dot_nt(a, b):
    return lax.dot_general(a, b, (((1,), (1,)), ((), ())), preferred_element_type=F32)


def _dot_tn(a, b):
    return lax.dot_general(a, b, (((0,), (0,)), ((), ())), preferred_element_type=F32)


def _bf16_pieces(x, n):
    pieces = []
    for _ in range(n - 1):
        p = x.astype(BF16)
        pieces.append(p)
        x = x - p.astype(F32)
    pieces.append(x.astype(BF16))
    return pieces


def _dot_exact_lhs(m16, x, n):
    ps = _bf16_pieces(x, n)
    acc = _dot(m16, ps[-1])
    for p in ps[-2::-1]:
        acc = acc + _dot(m16, p)
    return acc


def _dot_x3(a, b):
    ah, al = _bf16_pieces(a, 2)
    bh, bl = _bf16_pieces(b, 2)
    return _dot(ah, bh) + (_dot(al, bh) + _dot(ah, bl))


def _sigmoid(x):
    return 1.0 / (1.0 + jnp.exp(-x))


def _softplus(x):
    return jnp.maximum(x, 0.0) + jnp.log(1.0 + jnp.exp(-jnp.abs(x)))


def _mm_body(*refs, n_x, norm, eps, residual, sub):
    refs = list(refs)
    x_refs = [refs.pop(0) for _ in range(n_x)]
    g_ref = refs.pop(0) if norm else None
    w_ref = refs.pop(0)
    r_ref = refs.pop(0) if residual else None
    o_ref, xs_ref = refs

    @pl.when(pl.program_id(1) == 0)
    def _():
        tm = xs_ref.shape[0]
        col = 0
        for x_ref in x_refs:
            width = x_ref.shape[1]
            for s in range(0, tm, sub):
                x = x_ref[s:s + sub, :]
                if norm:
                    ms = jnp.mean(x * x, axis=-1, keepdims=True)
                    x = x * lax.rsqrt(ms + eps) * g_ref[...]
                xs_ref[s:s + sub, col:col + width] = x.astype(BF16)
            col += width

    acc = _dot(xs_ref[...], w_ref[...].astype(BF16))
    if residual:
        acc = acc + r_ref[...]
    o_ref[...] = acc


def _row_tile(m):
    for t in (1032, 688, 512, 344, 256, 128, 64, 32, 16, 8):
        if m % t == 0:
            return t
    raise ValueError(m)


def _matmul(x, w, layer, *, n_cols, col_off=0, tn=512, gain=None, eps=1e-6, residual=None):
    xs = x if isinstance(x, tuple) else (x,)
    m = xs[0].shape[0]
    k = sum(p.shape[1] for p in xs)
    tm = _row_tile(m)
    sub = 344 if tm % 344 == 0 else tm
    norm = gain is not None
    assert not norm or len(xs) == 1
    in_specs = [pl.BlockSpec((tm, p.shape[1]), lambda i, j: (i, 0)) for p in xs]
    args = list(xs)
    if norm:
        in_specs.append(pl.BlockSpec((None, 1, k), lambda i, j: (layer, 0, 0)))
        args.append(gain.reshape(gain.shape[0], 1, k))
    in_specs.append(pl.BlockSpec((None, k, tn), lambda i, j: (layer, 0, j + col_off)))
    args.append(w)
    if residual is not None:
        in_specs.append(pl.BlockSpec((tm, tn), lambda i, j: (i, j)))
        args.append(residual)
    return pl.pallas_call(
        functools.partial(_mm_body, n_x=len(xs), norm=norm, eps=eps, residual=residual is not None, sub=sub),
        grid=(m // tm, n_cols // tn),
        in_specs=in_specs,
        out_specs=pl.BlockSpec((tm, tn), lambda i, j: (i, j)),
        out_shape=jax.ShapeDtypeStruct((m, n_cols), F32),
        scratch_shapes=[pltpu.VMEM((tm, k), BF16)],
        compiler_params=pltpu.CompilerParams(
            dimension_semantics=("parallel", "arbitrary"), vmem_limit_bytes=VMEM_LIMIT),
        name="matmul",
    )(*args)


def _norm_body(x_ref, g_ref, o_ref, *, eps):
    x = x_ref[...]
    ms = jnp.mean(x * x, axis=-1, keepdims=True)
    o_ref[...] = x * lax.rsqrt(ms + eps) * g_ref[...]


def _rmsnorm(x3, gain, skip, eps=1e-6):
    if skip == 0:
        x3 = x3.reshape(1, -1, x3.shape[-1])
    b, t, k = x3.shape
    rows = t - skip
    tm = next(c for c in (256, 128, 64, 32, 16, 8) if rows % c == 0)
    return pl.pallas_call(
        functools.partial(_norm_body, eps=eps),
        grid=(b, rows // tm),
        in_specs=[pl.BlockSpec((pl.Element(1), pl.Element(tm), pl.Element(k)), lambda i, r: (i, pl.multiple_of(skip + r * tm, 8), 0)),
                  pl.BlockSpec((1, 1, k), lambda i, r: (0, 0, 0))],
        out_specs=pl.BlockSpec((1, tm, k), lambda i, r: (i, r, 0)),
        out_shape=jax.ShapeDtypeStruct((b, rows, k), F32),
        compiler_params=pltpu.CompilerParams(dimension_semantics=("parallel", "parallel")),
        name="final_norm",
    )(x3, gain.reshape(1, 1, k))


def _conv_body(ug_ref, uv_ref, bg_ref, bv_ref, wg_ref, wv_ref, cg_ref, cv_ref, o_ref):
    def conv(u_ref, buf_ref, w_ref, c_ref):
        u = u_ref[...]
        buf = buf_ref[...]
        w = w_ref[...]
        row = lax.broadcasted_iota(jnp.int32, (1, u.shape[1], 1), 1)
        b0 = buf[:, 0:1, :]
        b1 = buf[:, 1:2, :]
        u1 = jnp.where(row == 0, b1, pltpu.roll(u, 1, 1))
        u2 = jnp.where(row == 0, b0, jnp.where(row == 1, b1, pltpu.roll(u, 2, 1)))
        return c_ref[...] + u2 * w[0:1, :] + u1 * w[1:2, :] + u * w[2:3, :]

    gate = conv(ug_ref, bg_ref, wg_ref, cg_ref)
    val = conv(uv_ref, bv_ref, wv_ref, cv_ref)
    o_ref[...] = (gate * _sigmoid(gate) * val).astype(o_ref.dtype)


def _conv_gate(u3, buf, w_conv, b_conv, layer, *, tn=512):
    b, t, _ = u3.shape
    bb = b if t <= 64 else 1
    nj = D_FF // tn
    b_conv3 = b_conv.reshape(DEPTH, 1, 2 * D_FF)
    blk = lambda off: pl.BlockSpec((bb, t, tn), lambda i, j: (i, 0, j + off))
    bufs = lambda off: pl.BlockSpec((None, bb, CONV_W - 1, tn), lambda i, j: (layer, i, 0, j + off))
    ws = lambda off: pl.BlockSpec((None, CONV_W, tn), lambda i, j: (layer, 0, j + off))
    cs = lambda off: pl.BlockSpec((None, 1, tn), lambda i, j: (layer, 0, j + off))
    return pl.pallas_call(
        _conv_body,
        grid=(b // bb, nj),
        in_specs=[blk(0), blk(nj), bufs(0), bufs(nj), ws(0), ws(nj), cs(0), cs(nj)],
        out_specs=pl.BlockSpec((bb, t, tn), lambda i, j: (i, 0, j)),
        out_shape=jax.ShapeDtypeStruct((b, t, D_FF), BF16),
        compiler_params=pltpu.CompilerParams(
            dimension_semantics=("parallel", "parallel"), vmem_limit_bytes=VMEM_LIMIT),
        name="conv_gate",
    )(u3, u3, buf, buf, w_conv, w_conv, b_conv3, b_conv3)


def _up_conv_body(x_ref, g_ref, wg_ref, wv_ref, bg_ref, bv_ref, cwg_ref, cwv_ref, cbg_ref, cbv_ref,
                  act_ref, sg_ref, sv_ref, xs_ref, carry_ref, *, eps, sub, tiles_per_seq):
    i = pl.program_id(0)
    j = pl.program_id(1)
    tm = x_ref.shape[0]

    @pl.when((i == 0) & (j == 0))
    def _():
        carry_ref[...] = jnp.zeros_like(carry_ref)

    @pl.when(j == 0)
    def _():
        for s in range(0, tm, sub):
            x = x_ref[s:s + sub, :]
            ms = jnp.mean(x * x, axis=-1, keepdims=True)
            xs_ref[s:s + sub, :] = (x * lax.rsqrt(ms + eps) * g_ref[...]).astype(BF16)

    first = (i % tiles_per_seq) == 0
    row = lax.broadcasted_iota(jnp.int32, (tm, 1), 0)

    def conv(w_ref, buf_ref, cw_ref, cb_ref, s_ref, slot):
        u = _dot(xs_ref[...], w_ref[...].astype(BF16))
        before = jnp.where(first, buf_ref[0], carry_ref[j, slot])
        tail = u[tm - (CONV_W - 1):, :]
        carry_ref[j, slot] = tail
        s_ref[0] = tail
        cw = cw_ref[...]
        u1 = jnp.where(row == 0, before[1:2, :], pltpu.roll(u, 1, 0))
        u2 = jnp.where(row == 0, before[0:1, :], jnp.where(row == 1, before[1:2, :], pltpu.roll(u, 2, 0)))
        return cb_ref[...] + u2 * cw[0:1, :] + u1 * cw[1:2, :] + u * cw[2:3, :]

    gate = conv(wg_ref, bg_ref, cwg_ref, cbg_ref, sg_ref, 0)
    val = conv(wv_ref, bv_ref, cwv_ref, cbv_ref, sv_ref, 1)
    act_ref[...] = (gate * _sigmoid(gate) * val).astype(act_ref.dtype)


def _up_conv_gate(h, gain, w_up, buf, w_conv, b_conv, layer, t, *, tn=512, eps=1e-6):
    m, k = h.shape
    b = m // t
    tm = _row_tile(t)
    tiles_per_seq = t // tm
    nj = D_FF // tn
    b_conv3 = b_conv.reshape(DEPTH, 1, 2 * D_FF)
    wsp = lambda off: pl.BlockSpec((None, k, tn), lambda i, j: (layer, 0, j + off))
    bufs = lambda off: pl.BlockSpec((None, 1, CONV_W - 1, tn), lambda i, j: (layer, i // tiles_per_seq, 0, j + off))
    cws = lambda off: pl.BlockSpec((None, CONV_W, tn), lambda i, j: (layer, 0, j + off))
    cbs = lambda off: pl.BlockSpec((None, 1, tn), lambda i, j: (layer, 0, j + off))
    tail_spec = pl.BlockSpec((1, CONV_W - 1, tn), lambda i, j: (i, 0, j))
    act, sg, sv = pl.pallas_call(
        functools.partial(_up_conv_body, eps=eps, sub=344 if tm % 344 == 0 else tm, tiles_per_seq=tiles_per_seq),
        grid=(m // tm, nj),
        in_specs=[
            pl.BlockSpec((tm, k), lambda i, j: (i, 0)),
            pl.BlockSpec((None, 1, k), lambda i, j: (layer, 0, 0)),
            wsp(0), wsp(nj), bufs(0), bufs(nj), cws(0), cws(nj), cbs(0), cbs(nj),
        ],
        out_specs=[pl.BlockSpec((tm, tn), lambda i, j: (i, j)), tail_spec, tail_spec],
        out_shape=[
            jax.ShapeDtypeStruct((m, D_FF), BF16),
            jax.ShapeDtypeStruct((m // tm, CONV_W - 1, D_FF), F32),
            jax.ShapeDtypeStruct((m // tm, CONV_W - 1, D_FF), F32),
        ],
        scratch_shapes=[pltpu.VMEM((tm, k), BF16), pltpu.VMEM((nj, 2, CONV_W - 1, tn), F32)],
        compiler_params=pltpu.CompilerParams(
            dimension_semantics=("arbitrary", "arbitrary"), vmem_limit_bytes=VMEM_LIMIT),
        name="up_conv_gate",
    )(h, gain.reshape(DEPTH, 1, k), w_up, w_up, buf, buf, w_conv, w_conv, b_conv3, b_conv3)
    seq_end = slice(tiles_per_seq - 1, None, tiles_per_seq)
    return act, jnp.concatenate([sg[seq_end], sv[seq_end]], axis=-1)


def _seg_sum64(x, bd16):
    n = x.shape[0]
    st = jnp.concatenate(_bf16_pieces(x, 2), axis=0)
    outs = []
    for j in range(x.shape[1] // LANES):
        y = _dot(st[:, j * LANES:(j + 1) * LANES], bd16)
        outs.append(y[:n] + y[n:])
    return jnp.concatenate(outs, axis=1)


def _rwkv_body(x_ref, shift_ref, s0_ref, mu_ref, w0_ref, wa2_ref, a0_ref, g2_ref, kk_ref, ka_ref, rk_ref,
               lnw_ref, lnb_ref, y_ref, sout_ref, prev_scr, s_scr, *, nb, chunk, n_sq):
    c = pl.program_id(1)
    n = nb * chunk

    @pl.when(c == 0)
    def _():
        prev_scr[...] = shift_ref[...]
        zero = jnp.zeros((NA, NA), F32)
        for i in range(nb):
            for j in range(HEAD_PAIRS):
                top = jnp.concatenate([s0_ref[i, 2 * j], zero], axis=1)
                bot = jnp.concatenate([zero, s0_ref[i, 2 * j + 1]], axis=1)
                s_scr[i, j] = jnp.concatenate([top, bot], axis=0)

    x = x_ref[...].reshape(n, PA)
    row = lax.broadcasted_iota(jnp.int32, (n, 1), 0)
    prev = pltpu.roll(x, 1, 0)
    for i in range(nb):
        prev = jnp.where(row == i * chunk, prev_scr[i], prev)
        prev_scr[i] = x[(i + 1) * chunk - 1:(i + 1) * chunk, :]
    xm = x + (prev - x) * mu_ref[...]
    r = xm[:, :WA]
    k = xm[:, WA:2 * WA]
    v = xm[:, 2 * WA:3 * WA]
    wa_in = xm[:, 3 * WA:3 * WA + LANES]
    gd = xm[:, 3 * WA + LANES:]

    lane = lax.broadcasted_iota(jnp.int32, (1, LANES), 1)
    lo = lane < NA
    wa2 = wa2_ref[...]
    z = w0_ref[...] + _dot_x3(jnp.where(lo, jnp.tanh(wa_in), 0.0), wa2)
    w = -_softplus(-z) - 0.5
    logd = -jnp.exp(w)
    a = _sigmoid(a0_ref[...] + _dot(jnp.where(lo, 0.0, wa_in).astype(BF16), wa2.astype(BF16)))
    g = _dot(_sigmoid(gd).astype(BF16), g2_ref[...].astype(BF16))

    ri = lax.broadcasted_iota(jnp.int32, (LANES, LANES), 0)
    ci = lax.broadcasted_iota(jnp.int32, (LANES, LANES), 1)
    bd_mask = (ri < NA) == (ci < NA)
    bd16 = jnp.where(bd_mask, 1.0, 0.0).astype(BF16)
    eye = ri == ci

    kkr = k * kk_ref[...]
    kk = kkr / jnp.maximum(jnp.sqrt(_seg_sum64(kkr * kkr, bd16)), 1e-12)
    k2 = k * (1.0 + (a - 1.0) * ka_ref[...])
    b = kk * a

    tn_ = lax.broadcasted_iota(jnp.int32, (n, n), 0)
    sn_ = lax.broadcasted_iota(jnp.int32, (n, n), 1)
    same = None
    for i in range(nb):
        blk = ((tn_ >= i * chunk) & (tn_ < (i + 1) * chunk) & (sn_ >= i * chunk) & (sn_ < (i + 1) * chunk))
        same = blk if same is None else (same | blk)
    tri_incl = jnp.where(same & (sn_ <= tn_), 1.0, 0.0).astype(BF16)
    tri_rest = jnp.where(same & (sn_ > tn_), 1.0, 0.0).astype(BF16)
    cl = _dot_exact_lhs(tri_incl, logd, 3)
    rl = _dot_exact_lhs(tri_rest, logd, 3)
    e_neg = jnp.exp(-cl)
    e_end = jnp.exp(rl)
    alpha = kk * jnp.exp(cl - logd)
    rho = r * jnp.exp(cl)
    beta = (b * e_neg).astype(BF16)
    kappa = (k2 * e_neg).astype(BF16)
    beta_e = (b * e_end).astype(BF16)
    kappa_e = (k2 * e_end).astype(BF16)
    v16 = v.astype(BF16)

    ti = lax.broadcasted_iota(jnp.int32, (chunk, chunk), 0)
    si = lax.broadcasted_iota(jnp.int32, (chunk, chunk), 1)
    incl = si <= ti
    strict = si < ti
    eye_c = jnp.where(ti == si, 1.0, 0.0).astype(F32)

    rows = lambda i: slice(i * chunk, (i + 1) * chunk)
    lanes = lambda j: slice(j * LANES, (j + 1) * LANES)
    pairs = [(i, j) for i in range(nb) for j in range(HEAD_PAIRS)]
    heads = [(i, j, hh) for (i, j) in pairs for hh in (0, 1)]

    a_p = {p: alpha[rows(p[0]), lanes(p[1])] for p in pairs}
    r_p = {p: rho[rows(p[0]), lanes(p[1])] for p in pairs}
    b_p = {p: beta[rows(p[0]), lanes(p[1])] for p in pairs}
    k_p = {p: kappa[rows(p[0]), lanes(p[1])] for p in pairs}
    v_p = {p: v16[rows(p[0]), lanes(p[1])] for p in pairs}
    ar_p = {p: jnp.concatenate([a_p[p], r_p[p]], axis=0) for p in pairs}

    arm = {}
    for (i, j, hh) in heads:
        m = lo if hh == 0 else jnp.logical_not(lo)
        arm[(i, j, hh)] = jnp.where(m, ar_p[(i, j)], 0.0).astype(BF16)
    mb = {h: _dot_nt(arm[h], b_p[h[:2]]) for h in heads}
    mk = {h: _dot_nt(arm[h], k_p[h[:2]]) for h in heads}
    mab = {h: jnp.where(strict, mb[h][:chunk], 0.0) for h in heads}
    mrb = {h: jnp.where(incl, mb[h][chunk:], 0.0).astype(BF16) for h in heads}
    mak = {h: jnp.where(strict, mk[h][:chunk], 0.0).astype(BF16) for h in heads}
    mrk = {h: jnp.where(incl, mk[h][chunk:], 0.0).astype(BF16) for h in heads}
    xh = {h: _dot(mak[h], v_p[h[:2]]) for h in heads}
    ov = {h: _dot(mrk[h], v_p[h[:2]]) for h in heads}
    pw = {h: (-mab[h]).astype(BF16) for h in heads}
    tinv = {h: eye_c - mab[h] for h in heads}
    for _ in range(n_sq):
        pw = {h: _dot(pw[h], pw[h]).astype(BF16) for h in heads}
        tinv = {h: tinv[h] + _dot(tinv[h].astype(BF16), pw[h]) for h in heads}
    wu = {h: _dot(tinv[h].astype(BF16), jnp.concatenate([a_p[h[:2]], xh[h]], axis=1).astype(BF16))
          for h in heads}
    y2 = {h: _dot(mrb[h], wu[h].astype(BF16)) for h in heads}

    def merge(f):
        return {p: jnp.where(lo, f((p[0], p[1], 0)), f((p[0], p[1], 1))) for p in pairs}

    w2 = merge(lambda h: wu[h][:, :LANES])
    u0 = merge(lambda h: wu[h][:, LANES:])
    rw = merge(lambda h: r_p[h[:2]] - y2[h][:, :LANES])
    o0 = merge(lambda h: ov[h] - y2[h][:, LANES:])

    s_old = {p: s_scr[p[0], p[1]] for p in pairs}
    s16 = {p: s_old[p].astype(BF16) for p in pairs}
    out = {p: _dot_nt(rw[p].astype(BF16), s16[p]) + o0[p] for p in pairs}
    w16 = {p: w2[p].astype(BF16) for p in pairs}
    be_p = {p: beta_e[rows(p[0]), lanes(p[1])] for p in pairs}
    ke_p = {p: kappa_e[rows(p[0]), lanes(p[1])] for p in pairs}
    wtb = {p: _dot_tn(w16[p], be_p[p]) for p in pairs}
    gm = {p: _dot_tn(jnp.concatenate([v_p[p], (-u0[p]).astype(BF16)], axis=0),
                     jnp.concatenate([ke_p[p], be_p[p]], axis=0)) for p in pairs}
    for p in pairs:
        i, j = p
        p_end = jnp.exp(cl[(i + 1) * chunk - 1:(i + 1) * chunk, lanes(j)])
        phi = jnp.where(bd_mask, jnp.where(eye, p_end, 0.0) - wtb[p], 0.0)
        s_scr[i, j] = _dot(s16[p], phi.astype(BF16)) + jnp.where(bd_mask, gm[p], 0.0)
    o = jnp.concatenate(
        [jnp.concatenate([out[(i, j)] for j in range(HEAD_PAIRS)], axis=1) for i in range(nb)], axis=0)

    mean = _seg_sum64(o, bd16) * (1.0 / NA)
    oc = o - mean
    var = _seg_sum64(oc * oc, bd16) * (1.0 / NA)
    o = oc * lax.rsqrt(var + 64e-5) * lnw_ref[...] + lnb_ref[...]
    o = o + _seg_sum64(r * k2 * rk_ref[...], bd16) * v
    y_ref[...] = (o * g).reshape(nb, chunk, WA).astype(y_ref.dtype)

    @pl.when(c == pl.num_programs(1) - 1)
    def _():
        for i in range(nb):
            for j in range(HEAD_PAIRS):
                s_pair = s_scr[i, j]
                sout_ref[i, 2 * j] = s_pair[:NA, :NA]
                sout_ref[i, 2 * j + 1] = s_pair[NA:, NA:]


def _n_squarings(chunk):
    n, reach = 0, 1
    while reach < chunk - 1:
        n += 1
        reach = 2 * reach + 1
    return n


def _rwkv(pa3, shift0, s0, params, layer, chunk, nb):
    b, t, _ = pa3.shape
    nc = t // chunk
    vec = lambda n: pl.BlockSpec((None, 1, n), lambda i, c: (layer, 0, 0))
    mat = lambda r, n: pl.BlockSpec((None, r, n), lambda i, c: (layer, 0, 0))
    y, s_out = pl.pallas_call(
        functools.partial(_rwkv_body, nb=nb, chunk=chunk, n_sq=_n_squarings(chunk)),
        grid=(b // nb, nc),
        in_specs=[
            pl.BlockSpec((nb, chunk, PA), lambda i, c: (i, c, 0)),
            pl.BlockSpec((nb, 1, PA), lambda i, c: (i, 0, 0)),
            pl.BlockSpec((nb, HA, NA, NA), lambda i, c: (i, 0, 0, 0)),
            vec(PA), vec(WA), mat(LANES, WA), vec(WA), mat(LORA_G, WA), vec(WA), vec(WA), vec(WA), vec(WA), vec(WA),
        ],
        out_specs=[
            pl.BlockSpec((nb, chunk, WA), lambda i, c: (i, c, 0)),
            pl.BlockSpec((nb, HA, NA, NA), lambda i, c: (i, 0, 0, 0)),
        ],
        out_shape=[
            jax.ShapeDtypeStruct((b, t, WA), BF16),
            jax.ShapeDtypeStruct((b, HA, NA, NA), F32),
        ],
        scratch_shapes=[pltpu.VMEM((nb, 1, PA), F32), pltpu.VMEM((nb, HEAD_PAIRS, LANES, LANES), F32)],
        compiler_params=pltpu.CompilerParams(
            dimension_semantics=("parallel", "arbitrary"), vmem_limit_bytes=VMEM_LIMIT),
        name="rwkv7",
    )(pa3, shift0.reshape(b, 1, PA), s0, *params)
    return y, s_out


def _hgrn_body(*refs, nb, chunk):
    pieces = refs[:4 * HG_PIECES]
    s0_ref, lb_ref, nw_ref, y_ref, sout_ref, s_scr = refs[4 * HG_PIECES:]
    group = lambda n, i: jnp.concatenate([r[i] for r in pieces[n * HG_PIECES:(n + 1) * HG_PIECES]], axis=1)
    c = pl.program_id(1)

    @pl.when(c == 0)
    def _():
        s_scr[...] = s0_ref[...]

    lb = lb_ref[...]
    ti = lax.broadcasted_iota(jnp.int32, (chunk, chunk), 0)
    si = lax.broadcasted_iota(jnp.int32, (chunk, chunk), 1)
    incl = si <= ti
    tri = jnp.where(incl, 1.0, 0.0).astype(BF16)
    mid = (chunk - 1) // 2
    sl = lambda h: slice(h * LANES, (h + 1) * LANES)

    qa, ka, qe, kl, iv, last = {}, {}, {}, {}, {}, {}
    for i in range(nb):
        q = group(0, i)
        fl = group(1, i)
        f = lb + (1.0 - lb) * _sigmoid(fl)
        log_f = jnp.log(jnp.maximum(f, F_FLOOR))
        key = (1.0 - lb) * _sigmoid(-fl)
        cum = _dot_exact_lhs(tri, log_f, 3)
        anchor = cum[mid:mid + 1, :]
        last[i] = cum[chunk - 1:chunk, :]
        qa[i] = (q * jnp.exp(cum - anchor)).astype(BF16)
        ka[i] = (key * jnp.exp(anchor - cum)).astype(BF16)
        qe[i] = (q * jnp.exp(cum)).astype(BF16)
        kl[i] = (key * jnp.exp(last[i] - cum)).astype(BF16)
        iv[i] = group(2, i).astype(BF16)

    chains = [(i, h) for i in range(nb) for h in range(HC)]
    s_old = {ch: s_scr[ch[0], ch[1]] for ch in chains}
    att = {(i, h): jnp.where(incl, _dot_nt(qa[i][:, sl(h)], ka[i][:, sl(h)]), 0.0).astype(BF16) for (i, h) in chains}
    inter = {(i, h): _dot(qe[i][:, sl(h)], s_old[(i, h)].astype(BF16)) for (i, h) in chains}
    upd = {(i, h): _dot_tn(kl[i][:, sl(h)], iv[i][:, sl(h)]) for (i, h) in chains}
    intra = {(i, h): _dot(att[(i, h)], iv[i][:, sl(h)]) for (i, h) in chains}
    for i in range(nb):
        outs = []
        for h in range(HC):
            o = intra[(i, h)] + inter[(i, h)]
            decay = jnp.exp(jnp.broadcast_to(last[i][:, sl(h)], (DKC, LANES)).T)
            s_scr[i, h] = decay * s_old[(i, h)] + upd[(i, h)]
            ms = jnp.mean(o * o, axis=-1, keepdims=True)
            outs.append(o * lax.rsqrt(ms + 1e-6) * nw_ref[...])
        g = group(3, i)
        y_ref[i] = (jnp.concatenate(outs, axis=1) * (g * _sigmoid(g))).astype(y_ref.dtype)

    @pl.when(c == pl.num_programs(1) - 1)
    def _():
        sout_ref[...] = s_scr[...]


def _hgrn(p3, s0, lb, norm_w, layer, chunk, nb):
    b, t, _ = p3.shape
    nc = t // chunk
    width = WC // HG_PIECES
    c0 = (PA + PB) // width
    col = lambda n: pl.BlockSpec((nb, chunk, width), lambda i, c: (i, c, c0 + n))
    y, s_out = pl.pallas_call(
        functools.partial(_hgrn_body, nb=nb, chunk=chunk),
        grid=(b // nb, nc),
        in_specs=[
            *[col(n) for n in range(4 * HG_PIECES)],
            pl.BlockSpec((nb, HC, DKC, DVC), lambda i, c: (i, 0, 0, 0)),
            pl.BlockSpec((None, 1, WC), lambda i, c: (layer, 0, 0)),
            pl.BlockSpec((None, 1, DVC), lambda i, c: (layer, 0, 0)),
        ],
        out_specs=[
            pl.BlockSpec((nb, chunk, WC), lambda i, c: (i, c, 0)),
            pl.BlockSpec((nb, HC, DKC, DVC), lambda i, c: (i, 0, 0, 0)),
        ],
        out_shape=[
            jax.ShapeDtypeStruct((b, t, WC), BF16),
            jax.ShapeDtypeStruct((b, HC, DKC, DVC), F32),
        ],
        scratch_shapes=[pltpu.VMEM((nb, HC, DKC, DVC), F32)],
        compiler_params=pltpu.CompilerParams(
            dimension_semantics=("parallel", "arbitrary"), vmem_limit_bytes=VMEM_LIMIT),
        name="hgrn2",
    )(*([p3] * (4 * HG_PIECES)), s0, lb.reshape(DEPTH, 1, WC), norm_w.reshape(DEPTH, 1, DVC))
    return y, s_out


def _split_maps(q, scale):
    lane = lax.broadcasted_iota(jnp.int32, (1, LANES), 1)
    q = q * scale
    return jnp.concatenate([jnp.where(lane < DKB, q, 0.0), jnp.where(lane < DKB, 0.0, q)], axis=0).astype(BF16)


def _subln(o, w, scale):
    ms = jnp.mean(o * o, axis=-1, keepdims=True)
    return o * lax.rsqrt(ms + 1e-5) * w * scale


def _attn_prompt_body(lam_ref, q_ref, k_ref, v_ref, w_ref, *refs, tq, nq, out_scale, n_carried):
    o_ref, ko_ref, vo_ref = refs[n_carried:]
    h = pl.program_id(1)
    qi = pl.program_id(2)
    t = k_ref.shape[1]
    lam = lam_ref[0]

    @pl.when(qi == 0)
    def _():
        for hh in range(HB):
            @pl.when(h == hh)
            def _(hh=hh):
                ko_ref[0, pl.ds(hh, t, stride=HB), :] = k_ref[0]
                vo_ref[0, pl.ds(hh, t, stride=HB), :] = v_ref[0]

    qq = _split_maps(q_ref[0], DKB ** -0.5)
    for n in range(nq):
        @pl.when(qi == n)
        def _(n=n):
            ext = (n + 1) * tq
            s = _dot_nt(qq, k_ref[0, :ext, :].astype(BF16))
            qpos = n * tq + lax.broadcasted_iota(jnp.int32, (tq, 1), 0)
            qpos = jnp.concatenate([qpos, qpos], axis=0)
            kpos = lax.broadcasted_iota(jnp.int32, (1, ext), 1)
            s = jnp.where(kpos <= qpos, s, MASK_VALUE)
            e = jnp.exp(s - jnp.max(s, axis=-1, keepdims=True))
            p = e * (1.0 / jnp.sum(e, axis=-1, keepdims=True))
            att = p[:tq] - lam * p[tq:]
            o = _dot(att.astype(BF16), v_ref[0, :ext, :].astype(BF16))
            o_ref[0] = _subln(o, w_ref[...], out_scale).astype(o_ref.dtype)


def _attn_prompt(p3, lam, subln_w, layer, out_scale, kv_all=None):
    b, t, _ = p3.shape
    tq = 344 if t % 344 == 0 else t
    c0 = PA // LANES
    kv_spec = pl.BlockSpec((None, 1, t * HB, LANES), lambda i, h, q: (layer, i, 0, 0))
    carried = () if kv_all is None else tuple(kv_all)
    n_in = 5
    yb, ko, vo = pl.pallas_call(
        functools.partial(_attn_prompt_body, tq=tq, nq=t // tq, out_scale=out_scale, n_carried=len(carried)),
        grid=(b, HB, t // tq),
        in_specs=[
            pl.BlockSpec(memory_space=pltpu.SMEM),
            pl.BlockSpec((1, tq, LANES), lambda i, h, q: (i, q, c0 + h)),
            pl.BlockSpec((1, t, LANES), lambda i, h, q: (i, 0, c0 + HB + h)),
            pl.BlockSpec((1, t, LANES), lambda i, h, q: (i, 0, c0 + 2 * HB + h)),
            pl.BlockSpec((None, 1, DVB), lambda i, h, q: (layer, 0, 0)),
        ] + [pl.BlockSpec(memory_space=pl.ANY)] * len(carried),
        out_specs=[pl.BlockSpec((1, tq, LANES), lambda i, h, q: (i, q, h)), kv_spec, kv_spec],
        out_shape=[
            jax.ShapeDtypeStruct((b, t, WB), BF16),
            jax.ShapeDtypeStruct((DEPTH, b, t * HB, 2 * DKB), F32),
            jax.ShapeDtypeStruct((DEPTH, b, t * HB, DVB), F32),
        ],
        input_output_aliases={n_in + n: 1 + n for n in range(len(carried))},
        compiler_params=pltpu.CompilerParams(
            dimension_semantics=("parallel", "arbitrary", "arbitrary"), vmem_limit_bytes=VMEM_LIMIT),
        name="diff_attn_prompt",
    )(lam, p3, p3, p3, subln_w.reshape(DEPTH, 1, DVB), *carried)
    return yb, (ko, vo)


def _attn_sample_body(pt_ref, lam_ref, *refs, n_pg, t, out_scale):
    del pt_ref
    q_ref, kn_ref, vn_ref, w_ref = refs[:4]
    k_refs = refs[4:4 + n_pg]
    v_refs = refs[4 + n_pg:4 + 2 * n_pg]
    o_ref, qq_scr, m_scr, l_scr, acc_scr = refs[4 + 2 * n_pg:]
    g = pl.program_id(1)
    hr = 2 * t
    page_rows = PAGE_SIZE * HB

    @pl.when(g == 0)
    def _():
        q = q_ref[0]
        qpos = lax.broadcasted_iota(jnp.int32, (t, 1), 0)
        qpos = jnp.concatenate([qpos, qpos], axis=0)
        kpos = lax.broadcasted_iota(jnp.int32, (1, t), 1)
        for h in range(HB):
            sl = slice(h * LANES, (h + 1) * LANES)
            rs = slice(h * hr, (h + 1) * hr)
            qq = _split_maps(q[:, sl], DKB ** -0.5)
            qq_scr[rs, :] = qq
            s = _dot_nt(qq, kn_ref[0][:, sl].astype(BF16))
            s = jnp.where(kpos <= qpos, s, MASK_VALUE)
            m = jnp.max(s, axis=-1, keepdims=True)
            e = jnp.exp(s - m)
            m_scr[rs, :] = m
            l_scr[rs, :] = jnp.sum(e, axis=-1, keepdims=True)
            acc_scr[rs, :] = _dot(e.astype(BF16), vn_ref[0][:, sl].astype(BF16))

    qq = qq_scr[...]
    rid = lax.broadcasted_iota(jnp.int32, (HB * hr, 1), 0)
    row_head = sum((rid >= h * hr).astype(jnp.int32) for h in range(1, HB))
    col_head = lax.broadcasted_iota(jnp.int32, (1, page_rows), 1) & (HB - 1)
    own = row_head == col_head
    s = jnp.concatenate(
        [jnp.where(own, _dot_nt(qq, kr[...].astype(BF16)), MASK_VALUE) for kr in k_refs], axis=1)
    m_old = m_scr[...]
    m_new = jnp.maximum(m_old, jnp.max(s, axis=-1, keepdims=True))
    e = jnp.exp(s - m_new).astype(BF16)
    corr = jnp.exp(m_old - m_new)
    pv = _dot(e[:, :page_rows], v_refs[0][...].astype(BF16))
    for n in range(1, n_pg):
        pv = pv + _dot(e[:, n * page_rows:(n + 1) * page_rows], v_refs[n][...].astype(BF16))
    l_scr[...] = corr * l_scr[...] + jnp.sum(e.astype(F32), axis=-1, keepdims=True)
    acc_scr[...] = corr * acc_scr[...] + pv
    m_scr[...] = m_new

    @pl.when(g == pl.num_programs(1) - 1)
    def _():
        lam = lam_ref[0]
        o = acc_scr[...] * (1.0 / l_scr[...])
        outs = []
        for h in range(HB):
            oh = o[h * hr:h * hr + t] - lam * o[h * hr + t:(h + 1) * hr]
            outs.append(_subln(oh, w_ref[...], out_scale))
        o_ref[0] = jnp.concatenate(outs, axis=1).astype(o_ref.dtype)


def _attn_sample(pb3, cache_k, cache_v, page_table, lam, subln_w, layer, out_scale):
    b, t, _ = pb3.shape
    n_pages = page_table.shape[1]
    n_pg = next(n for n in (16, 8, 4, 2, 1) if n_pages % n == 0)
    depth, n_pool = cache_k.shape[:2]
    assert HB & (HB - 1) == 0 and 2 * DKB == DVB == LANES
    ck = cache_k.reshape(depth, n_pool, PAGE_SIZE * HB, 2 * DKB)
    cv = cache_v.reshape(depth, n_pool, PAGE_SIZE * HB, DVB)

    def page_spec(n):
        return pl.BlockSpec((None, None, PAGE_SIZE * HB, LANES),
                            lambda i, g, pt: (layer, pt[i, g * n_pg + n], 0, 0))

    rows = HB * 2 * t
    c0 = PA // WB
    grid_spec = pltpu.PrefetchScalarGridSpec(
        num_scalar_prefetch=1,
        grid=(b, n_pages // n_pg),
        in_specs=[
            pl.BlockSpec(memory_space=pltpu.SMEM),
            pl.BlockSpec((1, t, WB), lambda i, g, pt: (i, 0, c0)),
            pl.BlockSpec((1, t, WB), lambda i, g, pt: (i, 0, c0 + 1)),
            pl.BlockSpec((1, t, WB), lambda i, g, pt: (i, 0, c0 + 2)),
            pl.BlockSpec((None, 1, DVB), lambda i, g, pt: (layer, 0, 0)),
        ] + [page_spec(n) for n in range(n_pg)] * 2,
        out_specs=pl.BlockSpec((1, t, WB), lambda i, g, pt: (i, 0, 0)),
        scratch_shapes=[
            pltpu.VMEM((rows, LANES), BF16), pltpu.VMEM((rows, 1), F32), pltpu.VMEM((rows, 1), F32),
            pltpu.VMEM((rows, DVB), F32)],
    )
    return pl.pallas_call(
        functools.partial(_attn_sample_body, n_pg=n_pg, t=t, out_scale=out_scale),
        grid_spec=grid_spec,
        out_shape=jax.ShapeDtypeStruct((b, t, WB), BF16),
        compiler_params=pltpu.CompilerParams(
            dimension_semantics=("parallel", "arbitrary"), vmem_limit_bytes=VMEM_LIMIT),
        name="diff_attn_sample",
    )(page_table, lam, pb3, pb3, pb3, subln_w.reshape(DEPTH, 1, DVB), *([ck] * n_pg), *([cv] * n_pg))


def _trunk(h3, paged, rwkv0, shift0, hgrn0, conv0, wt, chunk, nb, skip):
    b, t, _ = h3.shape
    m = b * t
    h = h3.reshape(m, D_MODEL)
    lbs = jax.nn.softmax(wt['hg_lb'].astype(F32), axis=0)
    lbs = jnp.cumsum(lbs, axis=0) - lbs[0]
    wa2 = jnp.concatenate([wt['rw_w2'], wt['rw_a2']], axis=1)
    row = lambda name, n: wt[name].reshape(DEPTH, 1, n)
    rw_params = (row('rw_mu', PA), row('rw_w0', WA), wa2, row('rw_a0', WA), wt['rw_g2'], row('rw_kk', WA),
                 row('rw_ka', WA), row('rw_rk', WA), row('rw_lnw', WA), row('rw_lnb', WA))
    ks, vs, sas, shs, scs, cvs = [], [], [], [], [], []
    kv_all = None
    for l in range(DEPTH):
        p3 = _matmul(h, wt['w_in'], l, n_cols=PA + PB + PC, gain=wt['norm1']).reshape(b, t, PA + PB + PC)
        ya, sa = _rwkv(p3, shift0[l], rwkv0[l], rw_params, l, chunk, nb)
        lam_init = 0.8 - 0.6 * math.exp(-0.3 * l)
        lam = (jnp.exp(jnp.sum(wt['da_lq1'][l].astype(F32) * wt['da_lk1'][l].astype(F32)))
               - jnp.exp(jnp.sum(wt['da_lq2'][l].astype(F32) * wt['da_lk2'][l].astype(F32))) + lam_init)
        lam = lam.reshape(1).astype(F32)
        if paged is None:
            yb, kv_all = _attn_prompt(p3, lam, wt['da_subln'], l, 1.0 - lam_init, kv_all)
        else:
            yb = _attn_sample(p3, paged[0], paged[1], paged[2], lam, wt['da_subln'], l, 1.0 - lam_init)
            ks.append(p3[:, :, PA + WB:PA + 2 * WB].reshape(b, t, HB, 2 * DKB))
            vs.append(p3[:, :, PA + 2 * WB:PA + 3 * WB].reshape(b, t, HB, DVB))
        yc, sc = _hgrn(p3, hgrn0[l], lbs, wt['hg_norm'], l, chunk, next(n for n in (4, 2, 1) if b % n == 0))
        mix = (ya.reshape(m, WA), yb.reshape(m, WB), yc.reshape(m, WC))
        h = _matmul(mix, wt['w_out'], l, n_cols=D_MODEL, residual=h)
        if t % 344 == 0:
            act, cb = _up_conv_gate(h, wt['norm2'], wt['ffn_up'], conv0, wt['ffn_conv'], wt['ffn_conv_b'], l, t)
        else:
            u3 = _matmul(h, wt['ffn_up'], l, n_cols=2 * D_FF, gain=wt['norm2']).reshape(b, t, 2 * D_FF)
            act = _conv_gate(u3, conv0, wt['ffn_conv'], wt['ffn_conv_b'], l).reshape(m, D_FF)
            cb = u3[:, t - (CONV_W - 1):, :]
        h = _matmul(act, wt['ffn_down'], l, n_cols=D_MODEL, tn=256, residual=h)
        sas.append(sa)
        shs.append(p3[:, t - 1, :PA])
        scs.append(sc)
        cvs.append(cb)
    y = _rmsnorm(h.reshape(b, t, D_MODEL), wt['final_norm'], skip).reshape(b, t - skip, D_MODEL)
    if paged is None:
        k_out = kv_all[0].reshape(DEPTH, b, t, HB, 2 * DKB)
        v_out = kv_all[1].reshape(DEPTH, b, t, HB, DVB)
    else:
        k_out, v_out = jnp.stack(ks), jnp.stack(vs)
    return (y, k_out, v_out, jnp.stack(sas), jnp.stack(shs), jnp.stack(scs), jnp.stack(cvs))


def kernel(x_prompt, x_sample, cache_k, cache_v, state_rwkv, state_shift, state_hgrn, state_conv, page_table, meta, norm1, w_in, rw_mu, rw_w0, rw_w2, rw_a0, rw_a2, rw_g2, rw_kk, rw_ka, rw_rk, rw_lnw, rw_lnb, da_lq1, da_lk1, da_lq2, da_lk2, da_subln, hg_lb, hg_norm, w_out, norm2, ffn_up, ffn_conv, ffn_conv_b, ffn_down, final_norm):
    wt = {'norm1': norm1, 'w_in': w_in, 'rw_mu': rw_mu, 'rw_w0': rw_w0, 'rw_w2': rw_w2, 'rw_a0': rw_a0,
          'rw_a2': rw_a2, 'rw_g2': rw_g2, 'rw_kk': rw_kk, 'rw_ka': rw_ka, 'rw_rk': rw_rk, 'rw_lnw': rw_lnw,
          'rw_lnb': rw_lnb, 'da_lq1': da_lq1, 'da_lk1': da_lk1, 'da_lq2': da_lq2, 'da_lk2': da_lk2,
          'da_subln': da_subln, 'hg_lb': hg_lb, 'hg_norm': hg_norm, 'w_out': w_out, 'norm2': norm2,
          'ffn_up': ffn_up, 'ffn_conv': ffn_conv, 'ffn_conv_b': ffn_conv_b, 'ffn_down': ffn_down,
          'final_norm': final_norm}
    for name in ('w_in', 'w_out', 'ffn_down'):
        wt[name] = wt[name].astype(BF16)
    bp = x_prompt.shape[0]
    dt = x_prompt.dtype
    hp = jnp.concatenate([jnp.broadcast_to(meta[None].astype(dt), (bp, N_META, D_MODEL)), x_prompt], axis=1)
    tp = hp.shape[1]
    chunk_p = next(c for c in (48, 24, 16, 8) if tp % c == 0)
    y_prompt, p_k, p_v, p_rwkv, p_shift, p_hgrn, p_conv = _trunk(
        hp, None,
        jnp.zeros((DEPTH, bp, HA, NA, NA), dt), jnp.zeros((DEPTH, bp, PA), dt),
        jnp.zeros((DEPTH, bp, HC, DKC, DVC), dt), jnp.zeros((DEPTH, bp, CONV_W - 1, 2 * D_FF), dt),
        wt, chunk_p, 2 if bp % 2 == 0 else 1, N_META)
    bs, ts = x_sample.shape[:2]
    y_sample, s_k, s_v, s_rwkv, s_shift, s_hgrn, s_conv = _trunk(
        x_sample, (cache_k, cache_v, page_table), state_rwkv, state_shift, state_hgrn, state_conv, wt, ts,
        4 if bs % 4 == 0 else 1, 0)
    return (y_prompt, y_sample, p_k, p_v, p_rwkv, p_shift, p_hgrn, p_conv,
            s_k, s_v, s_rwkv, s_shift, s_hgrn, s_conv)
```

```python
import functools
import math

import jax
import jax.numpy as jnp
from jax import lax
from jax.experimental import pallas as pl
from jax.experimental.pallas import tpu as pltpu

F32 = jnp.float32
BF16 = jnp.bfloat16

D_MODEL = 2048
DEPTH = 2
N_META = 16
PAGE_SIZE = 128
WB = D_MODEL // 4
WA = (D_MODEL - WB) // 2
WC = D_MODEL - WA - WB
NA = 64
HA = WA // NA
LORA_W = 64
LORA_A = 64
LORA_G = 128
DVB = 128
DKB = DVB // 2
HB = WB // DVB
MASK_VALUE = -1e30
DVC = 128
DKC = 128
HC = WC // DVC
F_FLOOR = 1e-30
D_FF = ((8 * D_MODEL // 3 + 255) // 256) * 256
CONV_W = 3
PA = 3 * WA + LORA_W + LORA_A + LORA_G
PB = 2 * HB * 2 * DKB + HB * DVB
PC = 2 * HC * DKC + 2 * WC

LANES = 128
HEAD_PAIRS = HA // 2
HG_PIECES = 3
VMEM_LIMIT = 56 * 1024 * 1024


def _dot(a, b):
    return jnp.dot(a, b, preferred_element_type=F32)


def _dot_nt(a, b):
    return lax.dot_general(a, b, (((1,), (1,)), ((), ())), preferred_element_type=F32)


def _dot_tn(a, b):
    return lax.dot_general(a, b, (((0,), (0,)), ((), ())), preferred_element_type=F32)


def _bf16_pieces(x, n):
    pieces = []
    for _ in range(n - 1):
        p = x.astype(BF16)
        pieces.append(p)
        x = x - p.astype(F32)
    pieces.append(x.astype(BF16))
    return pieces


def _dot_exact_lhs(m16, x, n):
    ps = _bf16_pieces(x, n)
    acc = _dot(m16, ps[-1])
    for p in ps[-2::-1]:
        acc = acc + _dot(m16, p)
    return acc


def _dot_x3(a, b):
    ah, al = _bf16_pieces(a, 2)
    bh, bl = _bf16_pieces(b, 2)
    return _dot(ah, bh) + (_dot(al, bh) + _dot(ah, bl))


def _sigmoid(x):
    return 1.0 / (1.0 + jnp.exp(-x))


def _softplus(x):
    return jnp.maximum(x, 0.0) + jnp.log(1.0 + jnp.exp(-jnp.abs(x)))


def _mm_body(*refs, n_x, norm, eps, residual, sub):
    refs = list(refs)
    x_refs = [refs.pop(0) for _ in range(n_x)]
    g_ref = refs.pop(0) if norm else None
    w_ref = refs.pop(0)
    r_ref = refs.pop(0) if residual else None
    o_ref, xs_ref = refs

    @pl.when(pl.program_id(1) == 0)
    def _():
        tm = xs_ref.shape[0]
        col = 0
        for x_ref in x_refs:
            width = x_ref.shape[1]
            for s in range(0, tm, sub):
                x = x_ref[s:s + sub, :]
                if norm:
                    ms = jnp.mean(x * x, axis=-1, keepdims=True)
                    x = x * lax.rsqrt(ms + eps) * g_ref[...]
                xs_ref[s:s + sub, col:col + width] = x.astype(BF16)
            col += width

    acc = _dot(xs_ref[...], w_ref[...].astype(BF16))
    if residual:
        acc = acc + r_ref[...]
    o_ref[...] = acc


def _row_tile(m):
    for t in (1032, 688, 512, 344, 256, 128, 64, 32, 16, 8):
        if m % t == 0:
            return t
    raise ValueError(m)


def _matmul(x, w, layer, *, n_cols, col_off=0, tn=512, gain=None, eps=1e-6, residual=None):
    xs = x if isinstance(x, tuple) else (x,)
    m = xs[0].shape[0]
    k = sum(p.shape[1] for p in xs)
    tm = _row_tile(m)
    sub = 344 if tm % 344 == 0 else tm
    norm = gain is not None
    assert not norm or len(xs) == 1
    in_specs = [pl.BlockSpec((tm, p.shape[1]), lambda i, j: (i, 0)) for p in xs]
    args = list(xs)
    if norm:
        in_specs.append(pl.BlockSpec((None, 1, k), lambda i, j: (layer, 0, 0)))
        args.append(gain.reshape(gain.shape[0], 1, k))
    in_specs.append(pl.BlockSpec((None, k, tn), lambda i, j: (layer, 0, j + col_off)))
    args.append(w)
    if residual is not None:
        in_specs.append(pl.BlockSpec((tm, tn), lambda i, j: (i, j)))
        args.append(residual)
    return pl.pallas_call(
        functools.partial(_mm_body, n_x=len(xs), norm=norm, eps=eps, residual=residual is not None, sub=sub),
        grid=(m // tm, n_cols // tn),
        in_specs=in_specs,
        out_specs=pl.BlockSpec((tm, tn), lambda i, j: (i, j)),
        out_shape=jax.ShapeDtypeStruct((m, n_cols), F32),
        scratch_shapes=[pltpu.VMEM((tm, k), BF16)],
        compiler_params=pltpu.CompilerParams(
            dimension_semantics=("parallel", "arbitrary"), vmem_limit_bytes=VMEM_LIMIT),
        name="matmul",
    )(*args)


def _norm_body(x_ref, g_ref, o_ref, *, eps):
    x = x_ref[...]
    ms = jnp.mean(x * x, axis=-1, keepdims=True)
    o_ref[...] = x * lax.rsqrt(ms + eps) * g_ref[...]


def _rmsnorm(x3, gain, skip, eps=1e-6):
    if skip == 0:
        x3 = x3.reshape(1, -1, x3.shape[-1])
    b, t, k = x3.shape
    rows = t - skip
    tm = next(c for c in (256, 128, 64, 32, 16, 8) if rows % c == 0)
    return pl.pallas_call(
        functools.partial(_norm_body, eps=eps),
        grid=(b, rows // tm),
        in_specs=[pl.BlockSpec((pl.Element(1), pl.Element(tm), pl.Element(k)), lambda i, r: (i, pl.multiple_of(skip + r * tm, 8), 0)),
                  pl.BlockSpec((1, 1, k), lambda i, r: (0, 0, 0))],
        out_specs=pl.BlockSpec((1, tm, k), lambda i, r: (i, r, 0)),
        out_shape=jax.ShapeDtypeStruct((b, rows, k), F32),
        compiler_params=pltpu.CompilerParams(dimension_semantics=("parallel", "parallel")),
        name="final_norm",
    )(x3, gain.reshape(1, 1, k))


def _conv_body(ug_ref, uv_ref, bg_ref, bv_ref, wg_ref, wv_ref, cg_ref, cv_ref, o_ref):
    def conv(u_ref, buf_ref, w_ref, c_ref):
        u = u_ref[...]
        buf = buf_ref[...]
        w = w_ref[...]
        row = lax.broadcasted_iota(jnp.int32, (1, u.shape[1], 1), 1)
        b0 = buf[:, 0:1, :]
        b1 = buf[:, 1:2, :]
        u1 = jnp.where(row == 0, b1, pltpu.roll(u, 1, 1))
        u2 = jnp.where(row == 0, b0, jnp.where(row == 1, b1, pltpu.roll(u, 2, 1)))
        return c_ref[...] + u2 * w[0:1, :] + u1 * w[1:2, :] + u * w[2:3, :]

    gate = conv(ug_ref, bg_ref, wg_ref, cg_ref)
    val = conv(uv_ref, bv_ref, wv_ref, cv_ref)
    o_ref[...] = (gate * _sigmoid(gate) * val).astype(o_ref.dtype)


def _conv_gate(u3, buf, w_conv, b_conv, layer, *, tn=512):
    b, t, _ = u3.shape
    bb = b if t <= 64 else 1
    nj = D_FF // tn
    b_conv3 = b_conv.reshape(DEPTH, 1, 2 * D_FF)
    blk = lambda off: pl.BlockSpec((bb, t, tn), lambda i, j: (i, 0, j + off))
    bufs = lambda off: pl.BlockSpec((None, bb, CONV_W - 1, tn), lambda i, j: (layer, i, 0, j + off))
    ws = lambda off: pl.BlockSpec((None, CONV_W, tn), lambda i, j: (layer, 0, j + off))
    cs = lambda off: pl.BlockSpec((None, 1, tn), lambda i, j: (layer, 0, j + off))
    return pl.pallas_call(
        _conv_body,
        grid=(b // bb, nj),
        in_specs=[blk(0), blk(nj), bufs(0), bufs(nj), ws(0), ws(nj), cs(0), cs(nj)],
        out_specs=pl.BlockSpec((bb, t, tn), lambda i, j: (i, 0, j)),
        out_shape=jax.ShapeDtypeStruct((b, t, D_FF), BF16),
        compiler_params=pltpu.CompilerParams(
            dimension_semantics=("parallel", "parallel"), vmem_limit_bytes=VMEM_LIMIT),
        name="conv_gate",
    )(u3, u3, buf, buf, w_conv, w_conv, b_conv3, b_conv3)


def _up_conv_body(x_ref, g_ref, wg_ref, wv_ref, bg_ref, bv_ref, cwg_ref, cwv_ref, cbg_ref, cbv_ref,
                  act_ref, sg_ref, sv_ref, xs_ref, carry_ref, *, eps, sub, tiles_per_seq):
    i = pl.program_id(0)
    j = pl.program_id(1)
    tm = x_ref.shape[0]

    @pl.when((i == 0) & (j == 0))
    def _():
        carry_ref[...] = jnp.zeros_like(carry_ref)

    @pl.when(j == 0)
    def _():
        for s in range(0, tm, sub):
            x = x_ref[s:s + sub, :]
            ms = jnp.mean(x * x, axis=-1, keepdims=True)
            xs_ref[s:s + sub, :] = (x * lax.rsqrt(ms + eps) * g_ref[...]).astype(BF16)

    first = (i % tiles_per_seq) == 0
    row = lax.broadcasted_iota(jnp.int32, (tm, 1), 0)

    def conv(w_ref, buf_ref, cw_ref, cb_ref, s_ref, slot):
        u = _dot(xs_ref[...], w_ref[...].astype(BF16))
        before = jnp.where(first, buf_ref[0], carry_ref[j, slot])
        tail = u[tm - (CONV_W - 1):, :]
        carry_ref[j, slot] = tail
        s_ref[0] = tail
        cw = cw_ref[...]
        u1 = jnp.where(row == 0, before[1:2, :], pltpu.roll(u, 1, 0))
        u2 = jnp.where(row == 0, before[0:1, :], jnp.where(row == 1, before[1:2, :], pltpu.roll(u, 2, 0)))
        return cb_ref[...] + u2 * cw[0:1, :] + u1 * cw[1:2, :] + u * cw[2:3, :]

    gate = conv(wg_ref, bg_ref, cwg_ref, cbg_ref, sg_ref, 0)
    val = conv(wv_ref, bv_ref, cwv_ref, cbv_ref, sv_ref, 1)
    act_ref[...] = (gate * _sigmoid(gate) * val).astype(act_ref.dtype)


def _up_conv_gate(h, gain, w_up, buf, w_conv, b_conv, layer, t, *, tn=512, eps=1e-6):
    m, k = h.shape
    b = m // t
    tm = _row_tile(t)
    tiles_per_seq = t // tm
    nj = D_FF // tn
    b_conv3 = b_conv.reshape(DEPTH, 1, 2 * D_FF)
    wsp = lambda off: pl.BlockSpec((None, k, tn), lambda i, j: (layer, 0, j + off))
    bufs = lambda off: pl.BlockSpec((None, 1, CONV_W - 1, tn), lambda i, j: (layer, i // tiles_per_seq, 0, j + off))
    cws = lambda off: pl.BlockSpec((None, CONV_W, tn), lambda i, j: (layer, 0, j + off))
    cbs = lambda off: pl.BlockSpec((None, 1, tn), lambda i, j: (layer, 0, j + off))
    tail_spec = pl.BlockSpec((1, CONV_W - 1, tn), lambda i, j: (i, 0, j))
    act, sg, sv = pl.pallas_call(
        functools.partial(_up_conv_body, eps=eps, sub=344 if tm % 344 == 0 else tm, tiles_per_seq=tiles_per_seq),
        grid=(m // tm, nj),
        in_specs=[
            pl.BlockSpec((tm, k), lambda i, j: (i, 0)),
            pl.BlockSpec((None, 1, k), lambda i, j: (layer, 0, 0)),
            wsp(0), wsp(nj), bufs(0), bufs(nj), cws(0), cws(nj), cbs(0), cbs(nj),
        ],
        out_specs=[pl.BlockSpec((tm, tn), lambda i, j: (i, j)), tail_spec, tail_spec],
        out_shape=[
            jax.ShapeDtypeStruct((m, D_FF), BF16),
            jax.ShapeDtypeStruct((m // tm, CONV_W - 1, D_FF), F32),
            jax.ShapeDtypeStruct((m // tm, CONV_W - 1, D_FF), F32),
        ],
        scratch_shapes=[pltpu.VMEM((tm, k), BF16), pltpu.VMEM((nj, 2, CONV_W - 1, tn), F32)],
        compiler_params=pltpu.CompilerParams(
            dimension_semantics=("arbitrary", "arbitrary"), vmem_limit_bytes=VMEM_LIMIT),
        name="up_conv_gate",
    )(h, gain.reshape(DEPTH, 1, k), w_up, w_up, buf, buf, w_conv, w_conv, b_conv3, b_conv3)
    seq_end = slice(tiles_per_seq - 1, None, tiles_per_seq)
    return act, jnp.concatenate([sg[seq_end], sv[seq_end]], axis=-1)


def _seg_sum64(x, bd16):
    n = x.shape[0]
    st = jnp.concatenate(_bf16_pieces(x, 2), axis=0)
    outs = []
    for j in range(x.shape[1] // LANES):
        y = _dot(st[:, j * LANES:(j + 1) * LANES], bd16)
        outs.append(y[:n] + y[n:])
    return jnp.concatenate(outs, axis=1)


def _rwkv_body(x_ref, shift_ref, s0_ref, mu_ref, w0_ref, wa2_ref, a0_ref, g2_ref, kk_ref, ka_ref, rk_ref,
               lnw_ref, lnb_ref, y_ref, sout_ref, prev_scr, s_scr, *, nb, chunk, n_sq):
    c = pl.program_id(1)
    n = nb * chunk

    @pl.when(c == 0)
    def _():
        prev_scr[...] = shift_ref[...]
        zero = jnp.zeros((NA, NA), F32)
        for i in range(nb):
            for j in range(HEAD_PAIRS):
                top = jnp.concatenate([s0_ref[i, 2 * j], zero], axis=1)
                bot = jnp.concatenate([zero, s0_ref[i, 2 * j + 1]], axis=1)
                s_scr[i, j] = jnp.concatenate([top, bot], axis=0)

    x = x_ref[...].reshape(n, PA)
    row = lax.broadcasted_iota(jnp.int32, (n, 1), 0)
    prev = pltpu.roll(x, 1, 0)
    for i in range(nb):
        prev = jnp.where(row == i * chunk, prev_scr[i], prev)
        prev_scr[i] = x[(i + 1) * chunk - 1:(i + 1) * chunk, :]
    xm = x + (prev - x) * mu_ref[...]
    r = xm[:, :WA]
    k = xm[:, WA:2 * WA]
    v = xm[:, 2 * WA:3 * WA]
    wa_in = xm[:, 3 * WA:3 * WA + LANES]
    gd = xm[:, 3 * WA + LANES:]

    lane = lax.broadcasted_iota(jnp.int32, (1, LANES), 1)
    lo = lane < NA
    wa2 = wa2_ref[...]
    z = w0_ref[...] + _dot_x3(jnp.where(lo, jnp.tanh(wa_in), 0.0), wa2)
    w = -_softplus(-z) - 0.5
    logd = -jnp.exp(w)
    a = _sigmoid(a0_ref[...] + _dot(jnp.where(lo, 0.0, wa_in).astype(BF16), wa2.astype(BF16)))
    g = _dot(_sigmoid(gd).astype(BF16), g2_ref[...].astype(BF16))

    ri = lax.broadcasted_iota(jnp.int32, (LANES, LANES), 0)
    ci = lax.broadcasted_iota(jnp.int32, (LANES, LANES), 1)
    bd_mask = (ri < NA) == (ci < NA)
    bd16 = jnp.where(bd_mask, 1.0, 0.0).astype(BF16)
    eye = ri == ci

    kkr = k * kk_ref[...]
    kk = kkr / jnp.maximum(jnp.sqrt(_seg_sum64(kkr * kkr, bd16)), 1e-12)
    k2 = k * (1.0 + (a - 1.0) * ka_ref[...])
    b = kk * a

    tn_ = lax.broadcasted_iota(jnp.int32, (n, n), 0)
    sn_ = lax.broadcasted_iota(jnp.int32, (n, n), 1)
    same = None
    for i in range(nb):
        blk = ((tn_ >= i * chunk) & (tn_ < (i + 1) * chunk) & (sn_ >= i * chunk) & (sn_ < (i + 1) * chunk))
        same = blk if same is None else (same | blk)
    tri_incl = jnp.where(same & (sn_ <= tn_), 1.0, 0.0).astype(BF16)
    tri_rest = jnp.where(same & (sn_ > tn_), 1.0, 0.0).astype(BF16)
    cl = _dot_exact_lhs(tri_incl, logd, 3)
    rl = _dot_exact_lhs(tri_rest, logd, 3)
    e_neg = jnp.exp(-cl)
    e_end = jnp.exp(rl)
    alpha = kk * jnp.exp(cl - logd)
    rho = r * jnp.exp(cl)
    beta = b * e_neg
    kappa = k2 * e_neg
    beta_e = (b * e_end).astype(BF16)
    kappa_e = (k2 * e_end).astype(BF16)
    v16 = v.astype(BF16)

    ti = lax.broadcasted_iota(jnp.int32, (chunk, 2 * chunk), 0)
    cz = lax.broadcasted_iota(jnp.int32, (chunk, 2 * chunk), 1)
    first = cz < chunk
    si = jnp.where(first, cz, cz - chunk)
    incl = si <= ti
    strict = si < ti
    eye_z = jnp.where(ti == si, 1.0, 0.0).astype(F32)
    lane2 = lax.broadcasted_iota(jnp.int32, (1, 2 * LANES), 1)
    lo2 = (lane2 < NA) | ((lane2 >= LANES) & (lane2 < LANES + NA))

    def by_head(x):
        m = lo if x.shape[1] == LANES else lo2
        return jnp.concatenate([jnp.where(m, x, 0.0), jnp.where(m, 0.0, x)], axis=0).astype(BF16)

    def block_diag(z):
        return jnp.concatenate([jnp.where(first, z, 0.0), jnp.where(first, 0.0, z)], axis=0).astype(BF16)

    rows = lambda i: slice(i * chunk, (i + 1) * chunk)
    lanes = lambda j: slice(j * LANES, (j + 1) * LANES)
    pairs = [(i, j) for i in range(nb) for j in range(HEAD_PAIRS)]
    cut = lambda x: {p: x[rows(p[0]), lanes(p[1])] for p in pairs}

    a_p, r_p, v_p = cut(alpha), cut(rho), cut(v16)
    ar = {p: jnp.concatenate([a_p[p], r_p[p]], axis=0).astype(BF16) for p in pairs}
    b_rows = {p: by_head(x) for p, x in cut(beta).items()}
    k_rows = {p: by_head(x) for p, x in cut(kappa).items()}
    v_rows = {p: by_head(x) for p, x in cut(v).items()}
    mb = {p: _dot_nt(ar[p], b_rows[p]) for p in pairs}
    mk = {p: _dot_nt(ar[p], k_rows[p]) for p in pairs}
    mab = {p: jnp.where(strict, mb[p][:chunk], 0.0) for p in pairs}
    mrb = {p: jnp.where(incl, mb[p][chunk:], 0.0).astype(BF16) for p in pairs}
    mak = {p: jnp.where(strict, mk[p][:chunk], 0.0).astype(BF16) for p in pairs}
    mrk = {p: jnp.where(incl, mk[p][chunk:], 0.0).astype(BF16) for p in pairs}
    xh = {p: _dot(mak[p], v_rows[p]) for p in pairs}
    ov = {p: _dot(mrk[p], v_rows[p]) for p in pairs}
    pw = {p: -mab[p] for p in pairs}
    tinv = {p: eye_z - mab[p] for p in pairs}
    pw_bd = {p: block_diag(pw[p]) for p in pairs}
    for _ in range(n_sq):
        pw = {p: _dot(pw[p].astype(BF16), pw_bd[p]) for p in pairs}
        pw_bd = {p: block_diag(pw[p]) for p in pairs}
        tinv = {p: tinv[p] + _dot(tinv[p].astype(BF16), pw_bd[p]) for p in pairs}
    wu = {p: _dot(tinv[p].astype(BF16), by_head(jnp.concatenate([a_p[p], xh[p]], axis=1))) for p in pairs}
    y2 = {p: _dot(mrb[p], by_head(wu[p])) for p in pairs}
    w2 = {p: wu[p][:, :LANES] for p in pairs}
    u0 = {p: wu[p][:, LANES:] for p in pairs}
    rw = {p: r_p[p] - y2[p][:, :LANES] for p in pairs}
    o0 = {p: ov[p] - y2[p][:, LANES:] for p in pairs}

    s_old = {p: s_scr[p[0], p[1]] for p in pairs}
    s16 = {p: s_old[p].astype(BF16) for p in pairs}
    out = {p: _dot_nt(rw[p].astype(BF16), s16[p]) + o0[p] for p in pairs}
    w16 = {p: w2[p].astype(BF16) for p in pairs}
    be_p = {p: beta_e[rows(p[0]), lanes(p[1])] for p in pairs}
    ke_p = {p: kappa_e[rows(p[0]), lanes(p[1])] for p in pairs}
    wtb = {p: _dot_tn(w16[p], be_p[p]) for p in pairs}
    gm = {p: _dot_tn(jnp.concatenate([v_p[p], (-u0[p]).astype(BF16)], axis=0),
                     jnp.concatenate([ke_p[p], be_p[p]], axis=0)) for p in pairs}
    for p in pairs:
        i, j = p
        p_end = jnp.exp(cl[(i + 1) * chunk - 1:(i + 1) * chunk, lanes(j)])
        phi = jnp.where(bd_mask, jnp.where(eye, p_end, 0.0) - wtb[p], 0.0)
        s_scr[i, j] = _dot(s16[p], phi.astype(BF16)) + jnp.where(bd_mask, gm[p], 0.0)
    o = jnp.concatenate(
        [jnp.concatenate([out[(i, j)] for j in range(HEAD_PAIRS)], axis=1) for i in range(nb)], axis=0)

    mean = _seg_sum64(o, bd16) * (1.0 / NA)
    oc = o - mean
    var = _seg_sum64(oc * oc, bd16) * (1.0 / NA)
    o = oc * lax.rsqrt(var + 64e-5) * lnw_ref[...] + lnb_ref[...]
    o = o + _seg_sum64(r * k2 * rk_ref[...], bd16) * v
    y_ref[...] = (o * g).reshape(nb, chunk, WA).astype(y_ref.dtype)

    @pl.when(c == pl.num_programs(1) - 1)
    def _():
        for i in range(nb):
            for j in range(HEAD_PAIRS):
                s_pair = s_scr[i, j]
                sout_ref[i, 2 * j] = s_pair[:NA, :NA]
                sout_ref[i, 2 * j + 1] = s_pair[NA:, NA:]


def _n_squarings(chunk):
    n, reach = 0, 1
    while reach < chunk - 1:
        n += 1
        reach = 2 * reach + 1
    return n


def _rwkv(pa3, shift0, s0, params, layer, chunk, nb):
    b, t, _ = pa3.shape
    nc = t // chunk
    vec = lambda n: pl.BlockSpec((None, 1, n), lambda i, c: (layer, 0, 0))
    mat = lambda r, n: pl.BlockSpec((None, r, n), lambda i, c: (layer, 0, 0))
    y, s_out = pl.pallas_call(
        functools.partial(_rwkv_body, nb=nb, chunk=chunk, n_sq=_n_squarings(chunk)),
        grid=(b // nb, nc),
        in_specs=[
            pl.BlockSpec((nb, chunk, PA), lambda i, c: (i, c, 0)),
            pl.BlockSpec((nb, 1, PA), lambda i, c: (i, 0, 0)),
            pl.BlockSpec((nb, HA, NA, NA), lambda i, c: (i, 0, 0, 0)),
            vec(PA), vec(WA), mat(LANES, WA), vec(WA), mat(LORA_G, WA), vec(WA), vec(WA), vec(WA), vec(WA), vec(WA),
        ],
        out_specs=[
            pl.BlockSpec((nb, chunk, WA), lambda i, c: (i, c, 0)),
            pl.BlockSpec((nb, HA, NA, NA), lambda i, c: (i, 0, 0, 0)),
        ],
        out_shape=[
            jax.ShapeDtypeStruct((b, t, WA), BF16),
            jax.ShapeDtypeStruct((b, HA, NA, NA), F32),
        ],
        scratch_shapes=[pltpu.VMEM((nb, 1, PA), F32), pltpu.VMEM((nb, HEAD_PAIRS, LANES, LANES), F32)],
        compiler_params=pltpu.CompilerParams(
            dimension_semantics=("parallel", "arbitrary"), vmem_limit_bytes=VMEM_LIMIT),
        name="rwkv7",
    )(pa3, shift0.reshape(b, 1, PA), s0, *params)
    return y, s_out


def _hgrn_body(*refs, nb, chunk):
    pieces = refs[:4 * HG_PIECES]
    s0_ref, lb_ref, nw_ref, y_ref, sout_ref, s_scr = refs[4 * HG_PIECES:]
    group = lambda n, i: jnp.concatenate([r[i] for r in pieces[n * HG_PIECES:(n + 1) * HG_PIECES]], axis=1)
    c = pl.program_id(1)

    @pl.when(c == 0)
    def _():
        s_scr[...] = s0_ref[...]

    lb = lb_ref[...]
    ti = lax.broadcasted_iota(jnp.int32, (chunk, chunk), 0)
    si = lax.broadcasted_iota(jnp.int32, (chunk, chunk), 1)
    incl = si <= ti
    tri = jnp.where(incl, 1.0, 0.0).astype(BF16)
    mid = (chunk - 1) // 2
    sl = lambda h: slice(h * LANES, (h + 1) * LANES)

    qa, ka, qe, kl, iv, last = {}, {}, {}, {}, {}, {}
    for i in range(nb):
        q = group(0, i)
        fl = group(1, i)
        f = lb + (1.0 - lb) * _sigmoid(fl)
        log_f = jnp.log(jnp.maximum(f, F_FLOOR))
        key = (1.0 - lb) * _sigmoid(-fl)
        cum = _dot_exact_lhs(tri, log_f, 3)
        anchor = cum[mid:mid + 1, :]
        last[i] = cum[chunk - 1:chunk, :]
        qa[i] = (q * jnp.exp(cum - anchor)).astype(BF16)
        ka[i] = (key * jnp.exp(anchor - cum)).astype(BF16)
        qe[i] = (q * jnp.exp(cum)).astype(BF16)
        kl[i] = (key * jnp.exp(last[i] - cum)).astype(BF16)
        iv[i] = group(2, i).astype(BF16)

    chains = [(i, h) for i in range(nb) for h in range(HC)]
    s_old = {ch: s_scr[ch[0], ch[1]] for ch in chains}
    att = {(i, h): jnp.where(incl, _dot_nt(qa[i][:, sl(h)], ka[i][:, sl(h)]), 0.0).astype(BF16) for (i, h) in chains}
    inter = {(i, h): _dot(qe[i][:, sl(h)], s_old[(i, h)].astype(BF16)) for (i, h) in chains}
    upd = {(i, h): _dot_tn(kl[i][:, sl(h)], iv[i][:, sl(h)]) for (i, h) in chains}
    intra = {(i, h): _dot(att[(i, h)], iv[i][:, sl(h)]) for (i, h) in chains}
    for i in range(nb):
        outs = []
        for h in range(HC):
            o = intra[(i, h)] + inter[(i, h)]
            decay = jnp.exp(jnp.broadcast_to(last[i][:, sl(h)], (DKC, LANES)).T)
            s_scr[i, h] = decay * s_old[(i, h)] + upd[(i, h)]
            ms = jnp.mean(o * o, axis=-1, keepdims=True)
            outs.append(o * lax.rsqrt(ms + 1e-6) * nw_ref[...])
        g = group(3, i)
        y_ref[i] = (jnp.concatenate(outs, axis=1) * (g * _sigmoid(g))).astype(y_ref.dtype)

    @pl.when(c == pl.num_programs(1) - 1)
    def _():
        sout_ref[...] = s_scr[...]


def _hgrn(p3, s0, lb, norm_w, layer, chunk, nb):
    b, t, _ = p3.shape
    nc = t // chunk
    width = WC // HG_PIECES
    c0 = (PA + PB) // width
    col = lambda n: pl.BlockSpec((nb, chunk, width), lambda i, c: (i, c, c0 + n))
    y, s_out = pl.pallas_call(
        functools.partial(_hgrn_body, nb=nb, chunk=chunk),
        grid=(b // nb, nc),
        in_specs=[
            *[col(n) for n in range(4 * HG_PIECES)],
            pl.BlockSpec((nb, HC, DKC, DVC), lambda i, c: (i, 0, 0, 0)),
            pl.BlockSpec((None, 1, WC), lambda i, c: (layer, 0, 0)),
            pl.BlockSpec((None, 1, DVC), lambda i, c: (layer, 0, 0)),
        ],
        out_specs=[
            pl.BlockSpec((nb, chunk, WC), lambda i, c: (i, c, 0)),
            pl.BlockSpec((nb, HC, DKC, DVC), lambda i, c: (i, 0, 0, 0)),
        ],
        out_shape=[
            jax.ShapeDtypeStruct((b, t, WC), BF16),
            jax.ShapeDtypeStruct((b, HC, DKC, DVC), F32),
        ],
        scratch_shapes=[pltpu.VMEM((nb, HC, DKC, DVC), F32)],
        compiler_params=pltpu.CompilerParams(
            dimension_semantics=("parallel", "arbitrary"), vmem_limit_bytes=VMEM_LIMIT),
        name="hgrn2",
    )(*([p3] * (4 * HG_PIECES)), s0, lb.reshape(DEPTH, 1, WC), norm_w.reshape(DEPTH, 1, DVC))
    return y, s_out


def _split_maps(q, scale):
    lane = lax.broadcasted_iota(jnp.int32, (1, LANES), 1)
    q = q * scale
    return jnp.concatenate([jnp.where(lane < DKB, q, 0.0), jnp.where(lane < DKB, 0.0, q)], axis=0).astype(BF16)


def _subln(o, w, scale):
    ms = jnp.mean(o * o, axis=-1, keepdims=True)
    return o * lax.rsqrt(ms + 1e-5) * w * scale


def _attn_prompt_body(lam_ref, q_ref, k_ref, v_ref, w_ref, *refs, tq, nq, out_scale, n_carried):
    o_ref, ko_ref, vo_ref = refs[n_carried:]
    h = pl.program_id(1)
    qi = pl.program_id(2)
    t = k_ref.shape[1]
    lam = lam_ref[0]

    @pl.when(qi == 0)
    def _():
        for hh in range(HB):
            @pl.when(h == hh)
            def _(hh=hh):
                ko_ref[0, pl.ds(hh, t, stride=HB), :] = k_ref[0]
                vo_ref[0, pl.ds(hh, t, stride=HB), :] = v_ref[0]

    qq = _split_maps(q_ref[0], DKB ** -0.5)
    for n in range(nq):
        @pl.when(qi == n)
        def _(n=n):
            ext = (n + 1) * tq
            s = _dot_nt(qq, k_ref[0, :ext, :].astype(BF16))
            qpos = n * tq + lax.broadcasted_iota(jnp.int32, (tq, 1), 0)
            qpos = jnp.concatenate([qpos, qpos], axis=0)
            kpos = lax.broadcasted_iota(jnp.int32, (1, ext), 1)
            s = jnp.where(kpos <= qpos, s, MASK_VALUE)
            e = jnp.exp(s - jnp.max(s, axis=-1, keepdims=True))
            p = e * (1.0 / jnp.sum(e, axis=-1, keepdims=True))
            att = p[:tq] - lam * p[tq:]
            o = _dot(att.astype(BF16), v_ref[0, :ext, :].astype(BF16))
            o_ref[0] = _subln(o, w_ref[...], out_scale).astype(o_ref.dtype)


def _attn_prompt(p3, lam, subln_w, layer, out_scale, kv_all=None):
    b, t, _ = p3.shape
    tq = 344 if t % 344 == 0 else t
    c0 = PA // LANES
    kv_spec = pl.BlockSpec((None, 1, t * HB, LANES), lambda i, h, q: (layer, i, 0, 0))
    carried = () if kv_all is None else tuple(kv_all)
    n_in = 5
    yb, ko, vo = pl.pallas_call(
        functools.partial(_attn_prompt_body, tq=tq, nq=t // tq, out_scale=out_scale, n_carried=len(carried)),
        grid=(b, HB, t // tq),
        in_specs=[
            pl.BlockSpec(memory_space=pltpu.SMEM),
            pl.BlockSpec((1, tq, LANES), lambda i, h, q: (i, q, c0 + h)),
            pl.BlockSpec((1, t, LANES), lambda i, h, q: (i, 0, c0 + HB + h)),
            pl.BlockSpec((1, t, LANES), lambda i, h, q: (i, 0, c0 + 2 * HB + h)),
            pl.BlockSpec((None, 1, DVB), lambda i, h, q: (layer, 0, 0)),
        ] + [pl.BlockSpec(memory_space=pl.ANY)] * len(carried),
        out_specs=[pl.BlockSpec((1, tq, LANES), lambda i, h, q: (i, q, h)), kv_spec, kv_spec],
        out_shape=[
            jax.ShapeDtypeStruct((b, t, WB), BF16),
            jax.ShapeDtypeStruct((DEPTH, b, t * HB, 2 * DKB), F32),
            jax.ShapeDtypeStruct((DEPTH, b, t * HB, DVB), F32),
        ],
        input_output_aliases={n_in + n: 1 + n for n in range(len(carried))},
        compiler_params=pltpu.CompilerParams(
            dimension_semantics=("parallel", "arbitrary", "arbitrary"), vmem_limit_bytes=VMEM_LIMIT),
        name="diff_attn_prompt",
    )(lam, p3, p3, p3, subln_w.reshape(DEPTH, 1, DVB), *carried)
    return yb, (ko, vo)


def _attn_sample_body(pt_ref, lam_ref, *refs, n_pg, t, out_scale):
    del pt_ref
    q_ref, kn_ref, vn_ref, w_ref = refs[:4]
    k_refs = refs[4:4 + n_pg]
    v_refs = refs[4 + n_pg:4 + 2 * n_pg]
    o_ref, qq_scr, m_scr, l_scr, acc_scr = refs[4 + 2 * n_pg:]
    g = pl.program_id(1)
    hr = 2 * t
    page_rows = PAGE_SIZE * HB

    @pl.when(g == 0)
    def _():
        q = q_ref[0]
        qpos = lax.broadcasted_iota(jnp.int32, (t, 1), 0)
        qpos = jnp.concatenate([qpos, qpos], axis=0)
        kpos = lax.broadcasted_iota(jnp.int32, (1, t), 1)
        for h in range(HB):
            sl = slice(h * LANES, (h + 1) * LANES)
            rs = slice(h * hr, (h + 1) * hr)
            qq = _split_maps(q[:, sl], DKB ** -0.5)
            qq_scr[rs, :] = qq
            s = _dot_nt(qq, kn_ref[0][:, sl].astype(BF16))
            s = jnp.where(kpos <= qpos, s, MASK_VALUE)
            m = jnp.max(s, axis=-1, keepdims=True)
            e = jnp.exp(s - m)
            m_scr[rs, :] = m
            l_scr[rs, :] = jnp.sum(e, axis=-1, keepdims=True)
            acc_scr[rs, :] = _dot(e.astype(BF16), vn_ref[0][:, sl].astype(BF16))

    qq = qq_scr[...]
    rid = lax.broadcasted_iota(jnp.int32, (HB * hr, 1), 0)
    row_head = sum((rid >= h * hr).astype(jnp.int32) for h in range(1, HB))
    col_head = lax.broadcasted_iota(jnp.int32, (1, page_rows), 1) & (HB - 1)
    own = row_head == col_head
    s = jnp.concatenate(
        [jnp.where(own, _dot_nt(qq, kr[...].astype(BF16)), MASK_VALUE) for kr in k_refs], axis=1)
    m_old = m_scr[...]
    m_new = jnp.maximum(m_old, jnp.max(s, axis=-1, keepdims=True))
    e = jnp.exp(s - m_new).astype(BF16)
    corr = jnp.exp(m_old - m_new)
    pv = _dot(e[:, :page_rows], v_refs[0][...].astype(BF16))
    for n in range(1, n_pg):
        pv = pv + _dot(e[:, n * page_rows:(n + 1) * page_rows], v_refs[n][...].astype(BF16))
    l_scr[...] = corr * l_scr[...] + jnp.sum(e.astype(F32), axis=-1, keepdims=True)
    acc_scr[...] = corr * acc_scr[...] + pv
    m_scr[...] = m_new

    @pl.when(g == pl.num_programs(1) - 1)
    def _():
        lam = lam_ref[0]
        o = acc_scr[...] * (1.0 / l_scr[...])
        outs = []
        for h in range(HB):
            oh = o[h * hr:h * hr + t] - lam * o[h * hr + t:(h + 1) * hr]
            outs.append(_subln(oh, w_ref[...], out_scale))
        o_ref[0] = jnp.concatenate(outs, axis=1).astype(o_ref.dtype)


def _attn_sample(pb3, cache_k, cache_v, page_table, lam, subln_w, layer, out_scale):
    b, t, _ = pb3.shape
    n_pages = page_table.shape[1]
    n_pg = next(n for n in (16, 8, 4, 2, 1) if n_pages % n == 0)
    depth, n_pool = cache_k.shape[:2]
    assert HB & (HB - 1) == 0 and 2 * DKB == DVB == LANES
    ck = cache_k.reshape(depth, n_pool, PAGE_SIZE * HB, 2 * DKB)
    cv = cache_v.reshape(depth, n_pool, PAGE_SIZE * HB, DVB)

    def page_spec(n):
        return pl.BlockSpec((None, None, PAGE_SIZE * HB, LANES),
                            lambda i, g, pt: (layer, pt[i, g * n_pg + n], 0, 0))

    rows = HB * 2 * t
    c0 = PA // WB
    grid_spec = pltpu.PrefetchScalarGridSpec(
        num_scalar_prefetch=1,
        grid=(b, n_pages // n_pg),
        in_specs=[
            pl.BlockSpec(memory_space=pltpu.SMEM),
            pl.BlockSpec((1, t, WB), lambda i, g, pt: (i, 0, c0)),
            pl.BlockSpec((1, t, WB), lambda i, g, pt: (i, 0, c0 + 1)),
            pl.BlockSpec((1, t, WB), lambda i, g, pt: (i, 0, c0 + 2)),
            pl.BlockSpec((None, 1, DVB), lambda i, g, pt: (layer, 0, 0)),
        ] + [page_spec(n) for n in range(n_pg)] * 2,
        out_specs=pl.BlockSpec((1, t, WB), lambda i, g, pt: (i, 0, 0)),
        scratch_shapes=[
            pltpu.VMEM((rows, LANES), BF16), pltpu.VMEM((rows, 1), F32), pltpu.VMEM((rows, 1), F32),
            pltpu.VMEM((rows, DVB), F32)],
    )
    return pl.pallas_call(
        functools.partial(_attn_sample_body, n_pg=n_pg, t=t, out_scale=out_scale),
        grid_spec=grid_spec,
        out_shape=jax.ShapeDtypeStruct((b, t, WB), BF16),
        compiler_params=pltpu.CompilerParams(
            dimension_semantics=("parallel", "arbitrary"), vmem_limit_bytes=VMEM_LIMIT),
        name="diff_attn_sample",
    )(page_table, lam, pb3, pb3, pb3, subln_w.reshape(DEPTH, 1, DVB), *([ck] * n_pg), *([cv] * n_pg))


def _trunk(h3, paged, rwkv0, shift0, hgrn0, conv0, wt, chunk, nb, skip):
    b, t, _ = h3.shape
    m = b * t
    h = h3.reshape(m, D_MODEL)
    lbs = jax.nn.softmax(wt['hg_lb'].astype(F32), axis=0)
    lbs = jnp.cumsum(lbs, axis=0) - lbs[0]
    wa2 = jnp.concatenate([wt['rw_w2'], wt['rw_a2']], axis=1)
    row = lambda name, n: wt[name].reshape(DEPTH, 1, n)
    rw_params = (row('rw_mu', PA), row('rw_w0', WA), wa2, row('rw_a0', WA), wt['rw_g2'], row('rw_kk', WA),
                 row('rw_ka', WA), row('rw_rk', WA), row('rw_lnw', WA), row('rw_lnb', WA))
    ks, vs, sas, shs, scs, cvs = [], [], [], [], [], []
    kv_all = None
    for l in range(DEPTH):
        p3 = _matmul(h, wt['w_in'], l, n_cols=PA + PB + PC, gain=wt['norm1']).reshape(b, t, PA + PB + PC)
        ya, sa = _rwkv(p3, shift0[l], rwkv0[l], rw_params, l, chunk, nb)
        lam_init = 0.8 - 0.6 * math.exp(-0.3 * l)
        lam = (jnp.exp(jnp.sum(wt['da_lq1'][l].astype(F32) * wt['da_lk1'][l].astype(F32)))
               - jnp.exp(jnp.sum(wt['da_lq2'][l].astype(F32) * wt['da_lk2'][l].astype(F32))) + lam_init)
        lam = lam.reshape(1).astype(F32)
        if paged is None:
            yb, kv_all = _attn_prompt(p3, lam, wt['da_subln'], l, 1.0 - lam_init, kv_all)
        else:
            yb = _attn_sample(p3, paged[0], paged[1], paged[2], lam, wt['da_subln'], l, 1.0 - lam_init)
            ks.append(p3[:, :, PA + WB:PA + 2 * WB].reshape(b, t, HB, 2 * DKB))
            vs.append(p3[:, :, PA + 2 * WB:PA + 3 * WB].reshape(b, t, HB, DVB))
        yc, sc = _hgrn(p3, hgrn0[l], lbs, wt['hg_norm'], l, chunk, next(n for n in (4, 2, 1) if b % n == 0))
        mix = (ya.reshape(m, WA), yb.reshape(m, WB), yc.reshape(m, WC))
        h = _matmul(mix, wt['w_out'], l, n_cols=D_MODEL, residual=h)
        if t % 344 == 0:
            act, cb = _up_conv_gate(h, wt['norm2'], wt['ffn_up'], conv0, wt['ffn_conv'], wt['ffn_conv_b'], l, t)
        else:
            u3 = _matmul(h, wt['ffn_up'], l, n_cols=2 * D_FF, gain=wt['norm2']).reshape(b, t, 2 * D_FF)
            act = _conv_gate(u3, conv0, wt['ffn_conv'], wt['ffn_conv_b'], l).reshape(m, D_FF)
            cb = u3[:, t - (CONV_W - 1):, :]
        h = _matmul(act, wt['ffn_down'], l, n_cols=D_MODEL, tn=256, residual=h)
        sas.append(sa)
        shs.append(p3[:, t - 1, :PA])
        scs.append(sc)
        cvs.append(cb)
    y = _rmsnorm(h.reshape(b, t, D_MODEL), wt['final_norm'], skip).reshape(b, t - skip, D_MODEL)
    if paged is None:
        k_out = kv_all[0].reshape(DEPTH, b, t, HB, 2 * DKB)
        v_out = kv_all[1].reshape(DEPTH, b, t, HB, DVB)
    else:
        k_out, v_out = jnp.stack(ks), jnp.stack(vs)
    return (y, k_out, v_out, jnp.stack(sas), jnp.stack(shs), jnp.stack(scs), jnp.stack(cvs))


def kernel(x_prompt, x_sample, cache_k, cache_v, state_rwkv, state_shift, state_hgrn, state_conv, page_table, meta, norm1, w_in, rw_mu, rw_w0, rw_w2, rw_a0, rw_a2, rw_g2, rw_kk, rw_ka, rw_rk, rw_lnw, rw_lnb, da_lq1, da_lk1, da_lq2, da_lk2, da_subln, hg_lb, hg_norm, w_out, norm2, ffn_up, ffn_conv, ffn_conv_b, ffn_down, final_norm):
    wt = {'norm1': norm1, 'w_in': w_in, 'rw_mu': rw_mu, 'rw_w0': rw_w0, 'rw_w2': rw_w2, 'rw_a0': rw_a0,
          'rw_a2': rw_a2, 'rw_g2': rw_g2, 'rw_kk': rw_kk, 'rw_ka': rw_ka, 'rw_rk': rw_rk, 'rw_lnw': rw_lnw,
          'rw_lnb': rw_lnb, 'da_lq1': da_lq1, 'da_lk1': da_lk1, 'da_lq2': da_lq2, 'da_lk2': da_lk2,
          'da_subln': da_subln, 'hg_lb': hg_lb, 'hg_norm': hg_norm, 'w_out': w_out, 'norm2': norm2,
          'ffn_up': ffn_up, 'ffn_conv': ffn_conv, 'ffn_conv_b': ffn_conv_b, 'ffn_down': ffn_down,
          'final_norm': final_norm}
    for name in ('w_in', 'w_out', 'ffn_down'):
        wt[name] = wt[name].astype(BF16)
    bp = x_prompt.shape[0]
    dt = x_prompt.dtype
    hp = jnp.concatenate([jnp.broadcast_to(meta[None].astype(dt), (bp, N_META, D_MODEL)), x_prompt], axis=1)
    tp = hp.shape[1]
    chunk_p = next(c for c in (48, 24, 16, 8) if tp % c == 0)
    y_prompt, p_k, p_v, p_rwkv, p_shift, p_hgrn, p_conv = _trunk(
        hp, None,
        jnp.zeros((DEPTH, bp, HA, NA, NA), dt), jnp.zeros((DEPTH, bp, PA), dt),
        jnp.zeros((DEPTH, bp, HC, DKC, DVC), dt), jnp.zeros((DEPTH, bp, CONV_W - 1, 2 * D_FF), dt),
        wt, chunk_p, next(n for n in (4, 2, 1) if bp % n == 0), N_META)
    bs, ts = x_sample.shape[:2]
    y_sample, s_k, s_v, s_rwkv, s_shift, s_hgrn, s_conv = _trunk(
        x_sample, (cache_k, cache_v, page_table), state_rwkv, state_shift, state_hgrn, state_conv, wt, ts,
        4 if bs % 4 == 0 else 1, 0)
    return (y_prompt, y_sample, p_k, p_v, p_rwkv, p_shift, p_hgrn, p_conv,
            s_k, s_v, s_rwkv, s_shift, s_hgrn, s_conv)
```

```python
import functools
import math

import jax
import jax.numpy as jnp
from jax import lax
from jax.experimental import pallas as pl
from jax.experimental.pallas import tpu as pltpu

F32 = jnp.float32
BF16 = jnp.bfloat16

D_MODEL = 2048
DEPTH = 2
N_META = 16
PAGE_SIZE = 128
WB = D_MODEL // 4
WA = (D_MODEL - WB) // 2
WC = D_MODEL - WA - WB
NA = 64
HA = WA // NA
LORA_W = 64
LORA_A = 64
LORA_G = 128
DVB = 128
DKB = DVB // 2
HB = WB // DVB
MASK_VALUE = -1e30
DVC = 128
DKC = 128
HC = WC // DVC
F_FLOOR = 1e-30
D_FF = ((8 * D_MODEL // 3 + 255) // 256) * 256
CONV_W = 3
PA = 3 * WA + LORA_W + LORA_A + LORA_G
PB = 2 * HB * 2 * DKB + HB * DVB
PC = 2 * HC * DKC + 2 * WC

LANES = 128
HEAD_PAIRS = HA // 2
HG_PIECES = 3
VMEM_LIMIT = 56 * 1024 * 1024


def _dot(a, b):
    return jnp.dot(a, b, preferred_element_type=F32)


def _dot_nt(a, b):
    return lax.dot_general(a, b, (((1,), (1,)), ((), ())), preferred_element_type=F32)


def _dot_tn(a, b):
    return lax.dot_general(a, b, (((0,), (0,)), ((), ())), preferred_element_type=F32)


def _bf16_pieces(x, n):
    pieces = []
    for _ in range(n - 1):
        p = x.astype(BF16)
        pieces.append(p)
        x = x - p.astype(F32)
    pieces.append(x.astype(BF16))
    return pieces


def _dot_exact_lhs(m16, x, n):
    ps = _bf16_pieces(x, n)
    acc = _dot(m16, ps[-1])
    for p in ps[-2::-1]:
        acc = acc + _dot(m16, p)
    return acc


def _dot_x3(a, b):
    ah, al = _bf16_pieces(a, 2)
    bh, bl = _bf16_pieces(b, 2)
    return _dot(ah, bh) + (_dot(al, bh) + _dot(ah, bl))


def _sigmoid(x):
    return 1.0 / (1.0 + jnp.exp(-x))


def _softplus(x):
    return jnp.maximum(x, 0.0) + jnp.log(1.0 + jnp.exp(-jnp.abs(x)))


def _mm_body(*refs, n_x, norm, eps, residual, sub):
    refs = list(refs)
    x_refs = [refs.pop(0) for _ in range(n_x)]
    g_ref = refs.pop(0) if norm else None
    w_ref = refs.pop(0)
    r_ref = refs.pop(0) if residual else None
    o_ref, xs_ref = refs

    @pl.when(pl.program_id(1) == 0)
    def _():
        tm = xs_ref.shape[0]
        col = 0
        for x_ref in x_refs:
            width = x_ref.shape[1]
            for s in range(0, tm, sub):
                x = x_ref[s:s + sub, :]
                if norm:
                    ms = jnp.mean(x * x, axis=-1, keepdims=True)
                    x = x * lax.rsqrt(ms + eps) * g_ref[...]
                xs_ref[s:s + sub, col:col + width] = x.astype(BF16)
            col += width

    acc = _dot(xs_ref[...], w_ref[...].astype(BF16))
    if residual:
        acc = acc + r_ref[...]
    o_ref[...] = acc


def _row_tile(m):
    for t in (1032, 688, 512, 344, 256, 128, 64, 32, 16, 8):
        if m % t == 0:
            return t
    raise ValueError(m)


def _matmul(x, w, layer, *, n_cols, col_off=0, tn=512, gain=None, eps=1e-6, residual=None):
    xs = x if isinstance(x, tuple) else (x,)
    m = xs[0].shape[0]
    k = sum(p.shape[1] for p in xs)
    tm = _row_tile(m)
    sub = 344 if tm % 344 == 0 else tm
    norm = gain is not None
    assert not norm or len(xs) == 1
    in_specs = [pl.BlockSpec((tm, p.shape[1]), lambda i, j: (i, 0)) for p in xs]
    args = list(xs)
    if norm:
        in_specs.append(pl.BlockSpec((None, 1, k), lambda i, j: (layer, 0, 0)))
        args.append(gain.reshape(gain.shape[0], 1, k))
    in_specs.append(pl.BlockSpec((None, k, tn), lambda i, j: (layer, 0, j + col_off)))
    args.append(w)
    if residual is not None:
        in_specs.append(pl.BlockSpec((tm, tn), lambda i, j: (i, j)))
        args.append(residual)
    return pl.pallas_call(
        functools.partial(_mm_body, n_x=len(xs), norm=norm, eps=eps, residual=residual is not None, sub=sub),
        grid=(m // tm, n_cols // tn),
        in_specs=in_specs,
        out_specs=pl.BlockSpec((tm, tn), lambda i, j: (i, j)),
        out_shape=jax.ShapeDtypeStruct((m, n_cols), F32),
        scratch_shapes=[pltpu.VMEM((tm, k), BF16)],
        compiler_params=pltpu.CompilerParams(
            dimension_semantics=("parallel", "arbitrary"), vmem_limit_bytes=VMEM_LIMIT),
        name="matmul",
    )(*args)


def _norm_body(x_ref, g_ref, o_ref, *, eps):
    x = x_ref[...]
    ms = jnp.mean(x * x, axis=-1, keepdims=True)
    o_ref[...] = x * lax.rsqrt(ms + eps) * g_ref[...]


def _rmsnorm(x3, gain, skip, eps=1e-6):
    if skip == 0:
        x3 = x3.reshape(1, -1, x3.shape[-1])
    b, t, k = x3.shape
    rows = t - skip
    tm = next(c for c in (256, 128, 64, 32, 16, 8) if rows % c == 0)
    return pl.pallas_call(
        functools.partial(_norm_body, eps=eps),
        grid=(b, rows // tm),
        in_specs=[pl.BlockSpec((pl.Element(1), pl.Element(tm), pl.Element(k)), lambda i, r: (i, pl.multiple_of(skip + r * tm, 8), 0)),
                  pl.BlockSpec((1, 1, k), lambda i, r: (0, 0, 0))],
        out_specs=pl.BlockSpec((1, tm, k), lambda i, r: (i, r, 0)),
        out_shape=jax.ShapeDtypeStruct((b, rows, k), F32),
        compiler_params=pltpu.CompilerParams(dimension_semantics=("parallel", "parallel")),
        name="final_norm",
    )(x3, gain.reshape(1, 1, k))


def _conv_body(ug_ref, uv_ref, bg_ref, bv_ref, wg_ref, wv_ref, cg_ref, cv_ref, o_ref):
    def conv(u_ref, buf_ref, w_ref, c_ref):
        u = u_ref[...]
        buf = buf_ref[...]
        w = w_ref[...]
        row = lax.broadcasted_iota(jnp.int32, (1, u.shape[1], 1), 1)
        b0 = buf[:, 0:1, :]
        b1 = buf[:, 1:2, :]
        u1 = jnp.where(row == 0, b1, pltpu.roll(u, 1, 1))
        u2 = jnp.where(row == 0, b0, jnp.where(row == 1, b1, pltpu.roll(u, 2, 1)))
        return c_ref[...] + u2 * w[0:1, :] + u1 * w[1:2, :] + u * w[2:3, :]

    gate = conv(ug_ref, bg_ref, wg_ref, cg_ref)
    val = conv(uv_ref, bv_ref, wv_ref, cv_ref)
    o_ref[...] = (gate * _sigmoid(gate) * val).astype(o_ref.dtype)


def _conv_gate(u3, buf, w_conv, b_conv, layer, *, tn=512):
    b, t, _ = u3.shape
    bb = b if t <= 64 else 1
    nj = D_FF // tn
    b_conv3 = b_conv.reshape(DEPTH, 1, 2 * D_FF)
    blk = lambda off: pl.BlockSpec((bb, t, tn), lambda i, j: (i, 0, j + off))
    bufs = lambda off: pl.BlockSpec((None, bb, CONV_W - 1, tn), lambda i, j: (layer, i, 0, j + off))
    ws = lambda off: pl.BlockSpec((None, CONV_W, tn), lambda i, j: (layer, 0, j + off))
    cs = lambda off: pl.BlockSpec((None, 1, tn), lambda i, j: (layer, 0, j + off))
    return pl.pallas_call(
        _conv_body,
        grid=(b // bb, nj),
        in_specs=[blk(0), blk(nj), bufs(0), bufs(nj), ws(0), ws(nj), cs(0), cs(nj)],
        out_specs=pl.BlockSpec((bb, t, tn), lambda i, j: (i, 0, j)),
        out_shape=jax.ShapeDtypeStruct((b, t, D_FF), BF16),
        compiler_params=pltpu.CompilerParams(
            dimension_semantics=("parallel", "parallel"), vmem_limit_bytes=VMEM_LIMIT),
        name="conv_gate",
    )(u3, u3, buf, buf, w_conv, w_conv, b_conv3, b_conv3)


def _up_conv_body(x_ref, g_ref, wg_ref, wv_ref, bg_ref, bv_ref, cwg_ref, cwv_ref, cbg_ref, cbv_ref,
                  act_ref, sg_ref, sv_ref, xs_ref, carry_ref, *, eps, sub, tiles_per_seq):
    i = pl.program_id(0)
    j = pl.program_id(1)
    tm = x_ref.shape[0]

    @pl.when((i == 0) & (j == 0))
    def _():
        carry_ref[...] = jnp.zeros_like(carry_ref)

    @pl.when(j == 0)
    def _():
        for s in range(0, tm, sub):
            x = x_ref[s:s + sub, :]
            ms = jnp.mean(x * x, axis=-1, keepdims=True)
            xs_ref[s:s + sub, :] = (x * lax.rsqrt(ms + eps) * g_ref[...]).astype(BF16)

    first = (i % tiles_per_seq) == 0
    row = lax.broadcasted_iota(jnp.int32, (tm, 1), 0)

    def conv(w_ref, buf_ref, cw_ref, cb_ref, s_ref, slot):
        u = _dot(xs_ref[...], w_ref[...].astype(BF16))
        before = jnp.where(first, buf_ref[0], carry_ref[j, slot])
        tail = u[tm - (CONV_W - 1):, :]
        carry_ref[j, slot] = tail
        s_ref[0] = tail
        cw = cw_ref[...]
        u1 = jnp.where(row == 0, before[1:2, :], pltpu.roll(u, 1, 0))
        u2 = jnp.where(row == 0, before[0:1, :], jnp.where(row == 1, before[1:2, :], pltpu.roll(u, 2, 0)))
        return cb_ref[...] + u2 * cw[0:1, :] + u1 * cw[1:2, :] + u * cw[2:3, :]

    gate = conv(wg_ref, bg_ref, cwg_ref, cbg_ref, sg_ref, 0)
    val = conv(wv_ref, bv_ref, cwv_ref, cbv_ref, sv_ref, 1)
    act_ref[...] = (gate * _sigmoid(gate) * val).astype(act_ref.dtype)


def _up_conv_gate(h, gain, w_up, buf, w_conv, b_conv, layer, t, *, tn=512, eps=1e-6):
    m, k = h.shape
    b = m // t
    tm = _row_tile(t)
    tiles_per_seq = t // tm
    nj = D_FF // tn
    b_conv3 = b_conv.reshape(DEPTH, 1, 2 * D_FF)
    wsp = lambda off: pl.BlockSpec((None, k, tn), lambda i, j: (layer, 0, j + off))
    bufs = lambda off: pl.BlockSpec((None, 1, CONV_W - 1, tn), lambda i, j: (layer, i // tiles_per_seq, 0, j + off))
    cws = lambda off: pl.BlockSpec((None, CONV_W, tn), lambda i, j: (layer, 0, j + off))
    cbs = lambda off: pl.BlockSpec((None, 1, tn), lambda i, j: (layer, 0, j + off))
    tail_spec = pl.BlockSpec((1, CONV_W - 1, tn), lambda i, j: (i, 0, j))
    act, sg, sv = pl.pallas_call(
        functools.partial(_up_conv_body, eps=eps, sub=344 if tm % 344 == 0 else tm, tiles_per_seq=tiles_per_seq),
        grid=(m // tm, nj),
        in_specs=[
            pl.BlockSpec((tm, k), lambda i, j: (i, 0)),
            pl.BlockSpec((None, 1, k), lambda i, j: (layer, 0, 0)),
            wsp(0), wsp(nj), bufs(0), bufs(nj), cws(0), cws(nj), cbs(0), cbs(nj),
        ],
        out_specs=[pl.BlockSpec((tm, tn), lambda i, j: (i, j)), tail_spec, tail_spec],
        out_shape=[
            jax.ShapeDtypeStruct((m, D_FF), BF16),
            jax.ShapeDtypeStruct((m // tm, CONV_W - 1, D_FF), F32),
            jax.ShapeDtypeStruct((m // tm, CONV_W - 1, D_FF), F32),
        ],
        scratch_shapes=[pltpu.VMEM((tm, k), BF16), pltpu.VMEM((nj, 2, CONV_W - 1, tn), F32)],
        compiler_params=pltpu.CompilerParams(
            dimension_semantics=("arbitrary", "arbitrary"), vmem_limit_bytes=VMEM_LIMIT),
        name="up_conv_gate",
    )(h, gain.reshape(DEPTH, 1, k), w_up, w_up, buf, buf, w_conv, w_conv, b_conv3, b_conv3)
    seq_end = slice(tiles_per_seq - 1, None, tiles_per_seq)
    return act, jnp.concatenate([sg[seq_end], sv[seq_end]], axis=-1)


def _seg_sum64(x, bd16):
    n = x.shape[0]
    st = jnp.concatenate(_bf16_pieces(x, 2), axis=0)
    outs = []
    for j in range(x.shape[1] // LANES):
        y = _dot(st[:, j * LANES:(j + 1) * LANES], bd16)
        outs.append(y[:n] + y[n:])
    return jnp.concatenate(outs, axis=1)


def _rwkv_body(x_ref, shift_ref, s0_ref, mu_ref, w0_ref, wa2_ref, a0_ref, g2_ref, kk_ref, ka_ref, rk_ref,
               lnw_ref, lnb_ref, y_ref, sout_ref, prev_scr, s_scr, *, nb, chunk, n_sq):
    c = pl.program_id(1)
    n = nb * chunk

    @pl.when(c == 0)
    def _():
        prev_scr[...] = shift_ref[...]
        zero = jnp.zeros((NA, NA), F32)
        for i in range(nb):
            for j in range(HEAD_PAIRS):
                top = jnp.concatenate([s0_ref[i, 2 * j], zero], axis=1)
                bot = jnp.concatenate([zero, s0_ref[i, 2 * j + 1]], axis=1)
                s_scr[i, j] = jnp.concatenate([top, bot], axis=0)

    x = x_ref[...].reshape(n, PA)
    row = lax.broadcasted_iota(jnp.int32, (n, 1), 0)
    prev = pltpu.roll(x, 1, 0)
    for i in range(nb):
        prev = jnp.where(row == i * chunk, prev_scr[i], prev)
        prev_scr[i] = x[(i + 1) * chunk - 1:(i + 1) * chunk, :]
    xm = x + (prev - x) * mu_ref[...]
    r = xm[:, :WA]
    k = xm[:, WA:2 * WA]
    v = xm[:, 2 * WA:3 * WA]
    wa_in = xm[:, 3 * WA:3 * WA + LANES]
    gd = xm[:, 3 * WA + LANES:]

    lane = lax.broadcasted_iota(jnp.int32, (1, LANES), 1)
    lo = lane < NA
    wa2 = wa2_ref[...]
    z = w0_ref[...] + _dot_x3(jnp.where(lo, jnp.tanh(wa_in), 0.0), wa2)
    w = -_softplus(-z) - 0.5
    logd = -jnp.exp(w)
    a = _sigmoid(a0_ref[...] + _dot(jnp.where(lo, 0.0, wa_in).astype(BF16), wa2.astype(BF16)))
    g = _dot(_sigmoid(gd).astype(BF16), g2_ref[...].astype(BF16))

    ri = lax.broadcasted_iota(jnp.int32, (LANES, LANES), 0)
    ci = lax.broadcasted_iota(jnp.int32, (LANES, LANES), 1)
    bd_mask = (ri < NA) == (ci < NA)
    bd16 = jnp.where(bd_mask, 1.0, 0.0).astype(BF16)
    eye = ri == ci

    kkr = k * kk_ref[...]
    kk = kkr / jnp.maximum(jnp.sqrt(_seg_sum64(kkr * kkr, bd16)), 1e-12)
    k2 = k * (1.0 + (a - 1.0) * ka_ref[...])
    b = kk * a

    tn_ = lax.broadcasted_iota(jnp.int32, (n, n), 0)
    sn_ = lax.broadcasted_iota(jnp.int32, (n, n), 1)
    same = None
    for i in range(nb):
        blk = ((tn_ >= i * chunk) & (tn_ < (i + 1) * chunk) & (sn_ >= i * chunk) & (sn_ < (i + 1) * chunk))
        same = blk if same is None else (same | blk)
    tri_incl = jnp.where(same & (sn_ <= tn_), 1.0, 0.0).astype(BF16)
    tri_rest = jnp.where(same & (sn_ > tn_), 1.0, 0.0).astype(BF16)
    cl = _dot_exact_lhs(tri_incl, logd, 3)
    rl = _dot_exact_lhs(tri_rest, logd, 3)
    e_neg = jnp.exp(-cl)
    e_end = jnp.exp(rl)
    alpha = kk * jnp.exp(cl - logd)
    rho = r * jnp.exp(cl)
    beta = b * e_neg
    kappa = k2 * e_neg
    beta_e = (b * e_end).astype(BF16)
    kappa_e = (k2 * e_end).astype(BF16)
    v16 = v.astype(BF16)

    ti = lax.broadcasted_iota(jnp.int32, (chunk, 2 * chunk), 0)
    cz = lax.broadcasted_iota(jnp.int32, (chunk, 2 * chunk), 1)
    first = cz < chunk
    si = jnp.where(first, cz, cz - chunk)
    incl = si <= ti
    strict = si < ti
    eye_z = jnp.where(ti == si, 1.0, 0.0).astype(F32)
    lane2 = lax.broadcasted_iota(jnp.int32, (1, 2 * LANES), 1)
    lo2 = (lane2 < NA) | ((lane2 >= LANES) & (lane2 < LANES + NA))

    def by_head(x):
        m = lo if x.shape[1] == LANES else lo2
        return jnp.concatenate([jnp.where(m, x, 0.0), jnp.where(m, 0.0, x)], axis=0).astype(BF16)

    def block_diag(z):
        return jnp.concatenate([jnp.where(first, z, 0.0), jnp.where(first, 0.0, z)], axis=0).astype(BF16)

    rows = lambda i: slice(i * chunk, (i + 1) * chunk)
    lanes = lambda j: slice(j * LANES, (j + 1) * LANES)
    pairs = [(i, j) for i in range(nb) for j in range(HEAD_PAIRS)]
    cut = lambda x: {p: x[rows(p[0]), lanes(p[1])] for p in pairs}

    a_p, r_p, v_p = cut(alpha), cut(rho), cut(v16)
    ar = {p: jnp.concatenate([a_p[p], r_p[p]], axis=0).astype(BF16) for p in pairs}
    b_rows = {p: by_head(x) for p, x in cut(beta).items()}
    k_rows = {p: by_head(x) for p, x in cut(kappa).items()}
    v_rows = {p: by_head(x) for p, x in cut(v).items()}
    mb = {p: _dot_nt(ar[p], b_rows[p]) for p in pairs}
    mk = {p: _dot_nt(ar[p], k_rows[p]) for p in pairs}
    mab = {p: jnp.where(strict, mb[p][:chunk], 0.0) for p in pairs}
    mrb = {p: jnp.where(incl, mb[p][chunk:], 0.0).astype(BF16) for p in pairs}
    mak = {p: jnp.where(strict, mk[p][:chunk], 0.0).astype(BF16) for p in pairs}
    mrk = {p: jnp.where(incl, mk[p][chunk:], 0.0).astype(BF16) for p in pairs}
    xh = {p: _dot(mak[p], v_rows[p]) for p in pairs}
    ov = {p: _dot(mrk[p], v_rows[p]) for p in pairs}
    pw = {p: -mab[p] for p in pairs}
    tinv = {p: eye_z - mab[p] for p in pairs}
    pw_bd = {p: block_diag(pw[p]) for p in pairs}
    for _ in range(n_sq):
        pw = {p: _dot(pw[p].astype(BF16), pw_bd[p]) for p in pairs}
        pw_bd = {p: block_diag(pw[p]) for p in pairs}
        tinv = {p: tinv[p] + _dot(tinv[p].astype(BF16), pw_bd[p]) for p in pairs}
    wu = {p: _dot(tinv[p].astype(BF16), by_head(jnp.concatenate([a_p[p], xh[p]], axis=1))) for p in pairs}
    y2 = {p: _dot(mrb[p], by_head(wu[p])) for p in pairs}
    w2 = {p: wu[p][:, :LANES] for p in pairs}
    u0 = {p: wu[p][:, LANES:] for p in pairs}
    rw = {p: r_p[p] - y2[p][:, :LANES] for p in pairs}
    o0 = {p: ov[p] - y2[p][:, LANES:] for p in pairs}

    s_old = {p: s_scr[p[0], p[1]] for p in pairs}
    s16 = {p: s_old[p].astype(BF16) for p in pairs}
    out = {p: _dot_nt(rw[p].astype(BF16), s16[p]) + o0[p] for p in pairs}
    w16 = {p: w2[p].astype(BF16) for p in pairs}
    be_p = {p: beta_e[rows(p[0]), lanes(p[1])] for p in pairs}
    ke_p = {p: kappa_e[rows(p[0]), lanes(p[1])] for p in pairs}
    wtb = {p: _dot_tn(w16[p], be_p[p]) for p in pairs}
    gm = {p: _dot_tn(jnp.concatenate([v_p[p], (-u0[p]).astype(BF16)], axis=0),
                     jnp.concatenate([ke_p[p], be_p[p]], axis=0)) for p in pairs}
    for p in pairs:
        i, j = p
        p_end = jnp.exp(cl[(i + 1) * chunk - 1:(i + 1) * chunk, lanes(j)])
        phi = jnp.where(bd_mask, jnp.where(eye, p_end, 0.0) - wtb[p], 0.0)
        s_scr[i, j] = _dot(s16[p], phi.astype(BF16)) + jnp.where(bd_mask, gm[p], 0.0)
    o = jnp.concatenate(
        [jnp.concatenate([out[(i, j)] for j in range(HEAD_PAIRS)], axis=1) for i in range(nb)], axis=0)

    mean = _seg_sum64(o, bd16) * (1.0 / NA)
    oc = o - mean
    var = _seg_sum64(oc * oc, bd16) * (1.0 / NA)
    o = oc * lax.rsqrt(var + 64e-5) * lnw_ref[...] + lnb_ref[...]
    o = o + _seg_sum64(r * k2 * rk_ref[...], bd16) * v
    y_ref[...] = (o * g).reshape(nb, chunk, WA).astype(y_ref.dtype)

    @pl.when(c == pl.num_programs(1) - 1)
    def _():
        for i in range(nb):
            for j in range(HEAD_PAIRS):
                s_pair = s_scr[i, j]
                sout_ref[i, 2 * j] = s_pair[:NA, :NA]
                sout_ref[i, 2 * j + 1] = s_pair[NA:, NA:]


def _n_squarings(chunk):
    n, reach = 0, 1
    while reach < chunk - 1:
        n += 1
        reach = 2 * reach + 1
    return n


def _rwkv(pa3, shift0, s0, params, layer, chunk, nb):
    b, t, _ = pa3.shape
    nc = t // chunk
    vec = lambda n: pl.BlockSpec((None, 1, n), lambda i, c: (layer, 0, 0))
    mat = lambda r, n: pl.BlockSpec((None, r, n), lambda i, c: (layer, 0, 0))
    y, s_out = pl.pallas_call(
        functools.partial(_rwkv_body, nb=nb, chunk=chunk, n_sq=_n_squarings(chunk)),
        grid=(b // nb, nc),
        in_specs=[
            pl.BlockSpec((nb, chunk, PA), lambda i, c: (i, c, 0)),
            pl.BlockSpec((nb, 1, PA), lambda i, c: (i, 0, 0)),
            pl.BlockSpec((nb, HA, NA, NA), lambda i, c: (i, 0, 0, 0)),
            vec(PA), vec(WA), mat(LANES, WA), vec(WA), mat(LORA_G, WA), vec(WA), vec(WA), vec(WA), vec(WA), vec(WA),
        ],
        out_specs=[
            pl.BlockSpec((nb, chunk, WA), lambda i, c: (i, c, 0)),
            pl.BlockSpec((nb, HA, NA, NA), lambda i, c: (i, 0, 0, 0)),
        ],
        out_shape=[
            jax.ShapeDtypeStruct((b, t, WA), BF16),
            jax.ShapeDtypeStruct((b, HA, NA, NA), F32),
        ],
        scratch_shapes=[pltpu.VMEM((nb, 1, PA), F32), pltpu.VMEM((nb, HEAD_PAIRS, LANES, LANES), F32)],
        compiler_params=pltpu.CompilerParams(
            dimension_semantics=("parallel", "arbitrary"), vmem_limit_bytes=VMEM_LIMIT),
        name="rwkv7",
    )(pa3, shift0.reshape(b, 1, PA), s0, *params)
    return y, s_out


def _hgrn_body(*refs, nb, chunk):
    pieces = refs[:4 * HG_PIECES]
    s0_ref, lb_ref, nw_ref, y_ref, sout_ref, s_scr = refs[4 * HG_PIECES:]
    group = lambda n, i: jnp.concatenate([r[i] for r in pieces[n * HG_PIECES:(n + 1) * HG_PIECES]], axis=1)
    c = pl.program_id(1)

    @pl.when(c == 0)
    def _():
        s_scr[...] = s0_ref[...]

    lb = lb_ref[...]
    ti = lax.broadcasted_iota(jnp.int32, (chunk, chunk), 0)
    si = lax.broadcasted_iota(jnp.int32, (chunk, chunk), 1)
    incl = si <= ti
    tri = jnp.where(incl, 1.0, 0.0).astype(BF16)
    mid = (chunk - 1) // 2
    sl = lambda h: slice(h * LANES, (h + 1) * LANES)

    qa, ka, qe, kl, iv, last = {}, {}, {}, {}, {}, {}
    for i in range(nb):
        q = group(0, i)
        fl = group(1, i)
        f = lb + (1.0 - lb) * _sigmoid(fl)
        log_f = jnp.log(jnp.maximum(f, F_FLOOR))
        key = (1.0 - lb) * _sigmoid(-fl)
        cum = _dot_exact_lhs(tri, log_f, 3)
        anchor = cum[mid:mid + 1, :]
        last[i] = cum[chunk - 1:chunk, :]
        qa[i] = (q * jnp.exp(cum - anchor)).astype(BF16)
        ka[i] = (key * jnp.exp(anchor - cum)).astype(BF16)
        qe[i] = (q * jnp.exp(cum)).astype(BF16)
        kl[i] = (key * jnp.exp(last[i] - cum)).astype(BF16)
        iv[i] = group(2, i).astype(BF16)

    chains = [(i, h) for i in range(nb) for h in range(HC)]
    s_old = {ch: s_scr[ch[0], ch[1]] for ch in chains}
    att = {(i, h): jnp.where(incl, _dot_nt(qa[i][:, sl(h)], ka[i][:, sl(h)]), 0.0).astype(BF16) for (i, h) in chains}
    inter = {(i, h): _dot(qe[i][:, sl(h)], s_old[(i, h)].astype(BF16)) for (i, h) in chains}
    upd = {(i, h): _dot_tn(kl[i][:, sl(h)], iv[i][:, sl(h)]) for (i, h) in chains}
    intra = {(i, h): _dot(att[(i, h)], iv[i][:, sl(h)]) for (i, h) in chains}
    for i in range(nb):
        outs = []
        for h in range(HC):
            o = intra[(i, h)] + inter[(i, h)]
            decay = jnp.exp(jnp.broadcast_to(last[i][:, sl(h)], (DKC, LANES)).T)
            s_scr[i, h] = decay * s_old[(i, h)] + upd[(i, h)]
            ms = jnp.mean(o * o, axis=-1, keepdims=True)
            outs.append(o * lax.rsqrt(ms + 1e-6) * nw_ref[...])
        g = group(3, i)
        y_ref[i] = (jnp.concatenate(outs, axis=1) * (g * _sigmoid(g))).astype(y_ref.dtype)

    @pl.when(c == pl.num_programs(1) - 1)
    def _():
        sout_ref[...] = s_scr[...]


def _hgrn(p3, s0, lb, norm_w, layer, chunk, nb):
    b, t, _ = p3.shape
    nc = t // chunk
    width = WC // HG_PIECES
    c0 = (PA + PB) // width
    col = lambda n: pl.BlockSpec((nb, chunk, width), lambda i, c: (i, c, c0 + n))
    y, s_out = pl.pallas_call(
        functools.partial(_hgrn_body, nb=nb, chunk=chunk),
        grid=(b // nb, nc),
        in_specs=[
            *[col(n) for n in range(4 * HG_PIECES)],
            pl.BlockSpec((nb, HC, DKC, DVC), lambda i, c: (i, 0, 0, 0)),
            pl.BlockSpec((None, 1, WC), lambda i, c: (layer, 0, 0)),
            pl.BlockSpec((None, 1, DVC), lambda i, c: (layer, 0, 0)),
        ],
        out_specs=[
            pl.BlockSpec((nb, chunk, WC), lambda i, c: (i, c, 0)),
            pl.BlockSpec((nb, HC, DKC, DVC), lambda i, c: (i, 0, 0, 0)),
        ],
        out_shape=[
            jax.ShapeDtypeStruct((b, t, WC), BF16),
            jax.ShapeDtypeStruct((b, HC, DKC, DVC), F32),
        ],
        scratch_shapes=[pltpu.VMEM((nb, HC, DKC, DVC), F32)],
        compiler_params=pltpu.CompilerParams(
            dimension_semantics=("parallel", "arbitrary"), vmem_limit_bytes=VMEM_LIMIT),
        name="hgrn2",
    )(*([p3] * (4 * HG_PIECES)), s0, lb.reshape(DEPTH, 1, WC), norm_w.reshape(DEPTH, 1, DVC))
    return y, s_out


def _split_maps(q, scale):
    lane = lax.broadcasted_iota(jnp.int32, (1, LANES), 1)
    q = q * scale
    return jnp.concatenate([jnp.where(lane < DKB, q, 0.0), jnp.where(lane < DKB, 0.0, q)], axis=0).astype(BF16)


def _subln(o, w, scale):
    ms = jnp.mean(o * o, axis=-1, keepdims=True)
    return o * lax.rsqrt(ms + 1e-5) * w * scale


def _attn_prompt_body(lam_ref, q_ref, k_ref, v_ref, w_ref, *refs, tq, nq, out_scale, n_carried):
    o_ref, ko_ref, vo_ref = refs[n_carried:]
    h = pl.program_id(1)
    qi = pl.program_id(2)
    t = k_ref.shape[1]
    lam = lam_ref[0]

    @pl.when(qi == 0)
    def _():
        for hh in range(HB):
            @pl.when(h == hh)
            def _(hh=hh):
                ko_ref[0, pl.ds(hh, t, stride=HB), :] = k_ref[0]
                vo_ref[0, pl.ds(hh, t, stride=HB), :] = v_ref[0]

    qq = _split_maps(q_ref[0], DKB ** -0.5)
    for n in range(nq):
        @pl.when(qi == n)
        def _(n=n):
            ext = (n + 1) * tq
            s = _dot_nt(qq, k_ref[0, :ext, :].astype(BF16))
            qpos = n * tq + lax.broadcasted_iota(jnp.int32, (tq, 1), 0)
            qpos = jnp.concatenate([qpos, qpos], axis=0)
            kpos = lax.broadcasted_iota(jnp.int32, (1, ext), 1)
            s = jnp.where(kpos <= qpos, s, MASK_VALUE)
            e = jnp.exp(s - jnp.max(s, axis=-1, keepdims=True))
            p = e * (1.0 / jnp.sum(e, axis=-1, keepdims=True))
            att = p[:tq] - lam * p[tq:]
            o = _dot(att.astype(BF16), v_ref[0, :ext, :].astype(BF16))
            o_ref[0] = _subln(o, w_ref[...], out_scale).astype(o_ref.dtype)


def _attn_prompt(p3, lam, subln_w, layer, out_scale, kv_all=None):
    b, t, _ = p3.shape
    tq = 344 if t % 344 == 0 else t
    c0 = PA // LANES
    kv_spec = pl.BlockSpec((None, 1, t * HB, LANES), lambda i, h, q: (layer, i, 0, 0))
    carried = () if kv_all is None else tuple(kv_all)
    n_in = 5
    yb, ko, vo = pl.pallas_call(
        functools.partial(_attn_prompt_body, tq=tq, nq=t // tq, out_scale=out_scale, n_carried=len(carried)),
        grid=(b, HB, t // tq),
        in_specs=[
            pl.BlockSpec(memory_space=pltpu.SMEM),
            pl.BlockSpec((1, tq, LANES), lambda i, h, q: (i, q, c0 + h)),
            pl.BlockSpec((1, t, LANES), lambda i, h, q: (i, 0, c0 + HB + h)),
            pl.BlockSpec((1, t, LANES), lambda i, h, q: (i, 0, c0 + 2 * HB + h)),
            pl.BlockSpec((None, 1, DVB), lambda i, h, q: (layer, 0, 0)),
        ] + [pl.BlockSpec(memory_space=pl.ANY)] * len(carried),
        out_specs=[pl.BlockSpec((1, tq, LANES), lambda i, h, q: (i, q, h)), kv_spec, kv_spec],
        out_shape=[
            jax.ShapeDtypeStruct((b, t, WB), BF16),
            jax.ShapeDtypeStruct((DEPTH, b, t * HB, 2 * DKB), F32),
            jax.ShapeDtypeStruct((DEPTH, b, t * HB, DVB), F32),
        ],
        input_output_aliases={n_in + n: 1 + n for n in range(len(carried))},
        compiler_params=pltpu.CompilerParams(
            dimension_semantics=("parallel", "arbitrary", "arbitrary"), vmem_limit_bytes=VMEM_LIMIT),
        name="diff_attn_prompt",
    )(lam, p3, p3, p3, subln_w.reshape(DEPTH, 1, DVB), *carried)
    return yb, (ko, vo)


def _attn_sample_body(pt_ref, lam_ref, *refs, n_pg, n_grp, t, out_scale):
    del pt_ref
    q_ref, kn_ref, vn_ref, w_ref = refs[:4]
    k_refs = refs[4:4 + n_pg]
    v_refs = refs[4 + n_pg:4 + 2 * n_pg]
    o_ref, qq_scr, m_scr, l_scr, acc_scr = refs[4 + 2 * n_pg:]
    g = pl.program_id(1)
    hr = 2 * t
    page_rows = PAGE_SIZE * HB

    @pl.when(g == 0)
    def _():
        q = q_ref[0]
        qpos = lax.broadcasted_iota(jnp.int32, (t, 1), 0)
        qpos = jnp.concatenate([qpos, qpos], axis=0)
        kpos = lax.broadcasted_iota(jnp.int32, (1, t), 1)
        for h in range(HB):
            sl = slice(h * LANES, (h + 1) * LANES)
            rs = slice(h * hr, (h + 1) * hr)
            qq = _split_maps(q[:, sl], DKB ** -0.5)
            qq_scr[rs, :] = qq
            s = _dot_nt(qq, kn_ref[0][:, sl].astype(BF16))
            s = jnp.where(kpos <= qpos, s, MASK_VALUE)
            m = jnp.max(s, axis=-1, keepdims=True)
            e = jnp.exp(s - m)
            m_scr[0, rs, :] = m
            l_scr[0, rs, :] = jnp.sum(e, axis=-1, keepdims=True)
            acc_scr[0, rs, :] = _dot(e.astype(BF16), vn_ref[0][:, sl].astype(BF16))
        for grp in range(1, n_grp):
            m_scr[grp] = jnp.full(m_scr.shape[1:], MASK_VALUE, F32)
            l_scr[grp] = jnp.zeros(l_scr.shape[1:], F32)
            acc_scr[grp] = jnp.zeros(acc_scr.shape[1:], F32)

    qq = qq_scr[...]
    rid = lax.broadcasted_iota(jnp.int32, (HB * hr, 1), 0)
    row_head = sum((rid >= h * hr).astype(jnp.int32) for h in range(1, HB))
    col_head = lax.broadcasted_iota(jnp.int32, (1, page_rows), 1) & (HB - 1)
    own = row_head == col_head
    per = n_pg // n_grp
    groups = range(n_grp)
    pages = lambda grp: range(grp * per, (grp + 1) * per)
    s = {grp: jnp.concatenate([jnp.where(own, _dot_nt(qq, k_refs[n][...].astype(BF16)), MASK_VALUE)
                               for n in pages(grp)], axis=1) for grp in groups}
    m_old = {grp: m_scr[grp] for grp in groups}
    m_new = {grp: jnp.maximum(m_old[grp], jnp.max(s[grp], axis=-1, keepdims=True)) for grp in groups}
    e = {grp: jnp.exp(s[grp] - m_new[grp]).astype(BF16) for grp in groups}
    for grp in groups:
        corr = jnp.exp(m_old[grp] - m_new[grp])
        pv = None
        for idx, n in enumerate(pages(grp)):
            part = _dot(e[grp][:, idx * page_rows:(idx + 1) * page_rows], v_refs[n][...].astype(BF16))
            pv = part if pv is None else pv + part
        l_scr[grp] = corr * l_scr[grp] + jnp.sum(e[grp].astype(F32), axis=-1, keepdims=True)
        acc_scr[grp] = corr * acc_scr[grp] + pv
        m_scr[grp] = m_new[grp]

    @pl.when(g == pl.num_programs(1) - 1)
    def _():
        lam = lam_ref[0]
        m_all = m_scr[0]
        for grp in range(1, n_grp):
            m_all = jnp.maximum(m_all, m_scr[grp])
        l_all = jnp.zeros_like(m_all)
        acc_all = jnp.zeros(acc_scr.shape[1:], F32)
        for grp in groups:
            w_grp = jnp.exp(m_scr[grp] - m_all)
            l_all = l_all + w_grp * l_scr[grp]
            acc_all = acc_all + w_grp * acc_scr[grp]
        o = acc_all * (1.0 / l_all)
        outs = []
        for h in range(HB):
            oh = o[h * hr:h * hr + t] - lam * o[h * hr + t:(h + 1) * hr]
            outs.append(_subln(oh, w_ref[...], out_scale))
        o_ref[0] = jnp.concatenate(outs, axis=1).astype(o_ref.dtype)


def _attn_sample(pb3, cache_k, cache_v, page_table, lam, subln_w, layer, out_scale):
    b, t, _ = pb3.shape
    n_pages = page_table.shape[1]
    n_pg = next(n for n in (16, 8, 4, 2, 1) if n_pages % n == 0)
    n_grp = next(n for n in (4, 2, 1) if n_pg % n == 0)
    depth, n_pool = cache_k.shape[:2]
    assert HB & (HB - 1) == 0 and 2 * DKB == DVB == LANES
    ck = cache_k.reshape(depth, n_pool, PAGE_SIZE * HB, 2 * DKB)
    cv = cache_v.reshape(depth, n_pool, PAGE_SIZE * HB, DVB)

    def page_spec(n):
        return pl.BlockSpec((None, None, PAGE_SIZE * HB, LANES),
                            lambda i, g, pt: (layer, pt[i, g * n_pg + n], 0, 0))

    rows = HB * 2 * t
    c0 = PA // WB
    grid_spec = pltpu.PrefetchScalarGridSpec(
        num_scalar_prefetch=1,
        grid=(b, n_pages // n_pg),
        in_specs=[
            pl.BlockSpec(memory_space=pltpu.SMEM),
            pl.BlockSpec((1, t, WB), lambda i, g, pt: (i, 0, c0)),
            pl.BlockSpec((1, t, WB), lambda i, g, pt: (i, 0, c0 + 1)),
            pl.BlockSpec((1, t, WB), lambda i, g, pt: (i, 0, c0 + 2)),
            pl.BlockSpec((None, 1, DVB), lambda i, g, pt: (layer, 0, 0)),
        ] + [page_spec(n) for n in range(n_pg)] * 2,
        out_specs=pl.BlockSpec((1, t, WB), lambda i, g, pt: (i, 0, 0)),
        scratch_shapes=[
            pltpu.VMEM((rows, LANES), BF16), pltpu.VMEM((n_grp, rows, 1), F32), pltpu.VMEM((n_grp, rows, 1), F32),
            pltpu.VMEM((n_grp, rows, DVB), F32)],
    )
    return pl.pallas_call(
        functools.partial(_attn_sample_body, n_pg=n_pg, n_grp=n_grp, t=t, out_scale=out_scale),
        grid_spec=grid_spec,
        out_shape=jax.ShapeDtypeStruct((b, t, WB), BF16),
        compiler_params=pltpu.CompilerParams(
            dimension_semantics=("parallel", "arbitrary"), vmem_limit_bytes=VMEM_LIMIT),
        name="diff_attn_sample",
    )(page_table, lam, pb3, pb3, pb3, subln_w.reshape(DEPTH, 1, DVB), *([ck] * n_pg), *([cv] * n_pg))


def _trunk(h3, paged, rwkv0, shift0, hgrn0, conv0, wt, chunk, nb, skip):
    b, t, _ = h3.shape
    m = b * t
    h = h3.reshape(m, D_MODEL)
    lbs = jax.nn.softmax(wt['hg_lb'].astype(F32), axis=0)
    lbs = jnp.cumsum(lbs, axis=0) - lbs[0]
    wa2 = jnp.concatenate([wt['rw_w2'], wt['rw_a2']], axis=1)
    row = lambda name, n: wt[name].reshape(DEPTH, 1, n)
    rw_params = (row('rw_mu', PA), row('rw_w0', WA), wa2, row('rw_a0', WA), wt['rw_g2'], row('rw_kk', WA),
                 row('rw_ka', WA), row('rw_rk', WA), row('rw_lnw', WA), row('rw_lnb', WA))
    ks, vs, sas, shs, scs, cvs = [], [], [], [], [], []
    kv_all = None
    for l in range(DEPTH):
        p3 = _matmul(h, wt['w_in'], l, n_cols=PA + PB + PC, gain=wt['norm1']).reshape(b, t, PA + PB + PC)
        ya, sa = _rwkv(p3, shift0[l], rwkv0[l], rw_params, l, chunk, nb)
        lam_init = 0.8 - 0.6 * math.exp(-0.3 * l)
        lam = (jnp.exp(jnp.sum(wt['da_lq1'][l].astype(F32) * wt['da_lk1'][l].astype(F32)))
               - jnp.exp(jnp.sum(wt['da_lq2'][l].astype(F32) * wt['da_lk2'][l].astype(F32))) + lam_init)
        lam = lam.reshape(1).astype(F32)
        if paged is None:
            yb, kv_all = _attn_prompt(p3, lam, wt['da_subln'], l, 1.0 - lam_init, kv_all)
        else:
            yb = _attn_sample(p3, paged[0], paged[1], paged[2], lam, wt['da_subln'], l, 1.0 - lam_init)
            ks.append(p3[:, :, PA + WB:PA + 2 * WB].reshape(b, t, HB, 2 * DKB))
            vs.append(p3[:, :, PA + 2 * WB:PA + 3 * WB].reshape(b, t, HB, DVB))
        yc, sc = _hgrn(p3, hgrn0[l], lbs, wt['hg_norm'], l, chunk, next(n for n in (4, 2, 1) if b % n == 0))
        mix = (ya.reshape(m, WA), yb.reshape(m, WB), yc.reshape(m, WC))
        h = _matmul(mix, wt['w_out'], l, n_cols=D_MODEL, residual=h)
        if t % 344 == 0:
            act, cb = _up_conv_gate(h, wt['norm2'], wt['ffn_up'], conv0, wt['ffn_conv'], wt['ffn_conv_b'], l, t)
        else:
            u3 = _matmul(h, wt['ffn_up'], l, n_cols=2 * D_FF, gain=wt['norm2']).reshape(b, t, 2 * D_FF)
            act = _conv_gate(u3, conv0, wt['ffn_conv'], wt['ffn_conv_b'], l).reshape(m, D_FF)
            cb = u3[:, t - (CONV_W - 1):, :]
        h = _matmul(act, wt['ffn_down'], l, n_cols=D_MODEL, tn=256, residual=h)
        sas.append(sa)
        shs.append(p3[:, t - 1, :PA])
        scs.append(sc)
        cvs.append(cb)
    y = _rmsnorm(h.reshape(b, t, D_MODEL), wt['final_norm'], skip).reshape(b, t - skip, D_MODEL)
    if paged is None:
        k_out = kv_all[0].reshape(DEPTH, b, t, HB, 2 * DKB)
        v_out = kv_all[1].reshape(DEPTH, b, t, HB, DVB)
    else:
        k_out, v_out = jnp.stack(ks), jnp.stack(vs)
    return (y, k_out, v_out, jnp.stack(sas), jnp.stack(shs), jnp.stack(scs), jnp.stack(cvs))


def kernel(x_prompt, x_sample, cache_k, cache_v, state_rwkv, state_shift, state_hgrn, state_conv, page_table, meta, norm1, w_in, rw_mu, rw_w0, rw_w2, rw_a0, rw_a2, rw_g2, rw_kk, rw_ka, rw_rk, rw_lnw, rw_lnb, da_lq1, da_lk1, da_lq2, da_lk2, da_subln, hg_lb, hg_norm, w_out, norm2, ffn_up, ffn_conv, ffn_conv_b, ffn_down, final_norm):
    wt = {'norm1': norm1, 'w_in': w_in, 'rw_mu': rw_mu, 'rw_w0': rw_w0, 'rw_w2': rw_w2, 'rw_a0': rw_a0,
          'rw_a2': rw_a2, 'rw_g2': rw_g2, 'rw_kk': rw_kk, 'rw_ka': rw_ka, 'rw_rk': rw_rk, 'rw_lnw': rw_lnw,
          'rw_lnb': rw_lnb, 'da_lq1': da_lq1, 'da_lk1': da_lk1, 'da_lq2': da_lq2, 'da_lk2': da_lk2,
          'da_subln': da_subln, 'hg_lb': hg_lb, 'hg_norm': hg_norm, 'w_out': w_out, 'norm2': norm2,
          'ffn_up': ffn_up, 'ffn_conv': ffn_conv, 'ffn_conv_b': ffn_conv_b, 'ffn_down': ffn_down,
          'final_norm': final_norm}
    for name in ('w_in', 'w_out', 'ffn_down'):
        wt[name] = wt[name].astype(BF16)
    bp = x_prompt.shape[0]
    dt = x_prompt.dtype
    hp = jnp.concatenate([jnp.broadcast_to(meta[None].astype(dt), (bp, N_META, D_MODEL)), x_prompt], axis=1)
    tp = hp.shape[1]
    chunk_p = next(c for c in (48, 24, 16, 8) if tp % c == 0)
    y_prompt, p_k, p_v, p_rwkv, p_shift, p_hgrn, p_conv = _trunk(
        hp, None,
        jnp.zeros((DEPTH, bp, HA, NA, NA), dt), jnp.zeros((DEPTH, bp, PA), dt),
        jnp.zeros((DEPTH, bp, HC, DKC, DVC), dt), jnp.zeros((DEPTH, bp, CONV_W - 1, 2 * D_FF), dt),
        wt, chunk_p, next(n for n in (4, 2, 1) if bp % n == 0), N_META)
    bs, ts = x_sample.shape[:2]
    y_sample, s_k, s_v, s_rwkv, s_shift, s_hgrn, s_conv = _trunk(
        x_sample, (cache_k, cache_v, page_table), state_rwkv, state_shift, state_hgrn, state_conv, wt, ts,
        4 if bs % 4 == 0 else 1, 0)
    return (y_prompt, y_sample, p_k, p_v, p_rwkv, p_shift, p_hgrn, p_conv,
            s_k, s_v, s_rwkv, s_shift, s_hgrn, s_conv)
```

```python
import functools
import math

import jax
import jax.numpy as jnp
from jax import lax
from jax.experimental import pallas as pl
from jax.experimental.pallas import tpu as pltpu

F32 = jnp.float32
BF16 = jnp.bfloat16

D_MODEL = 2048
DEPTH = 2
N_META = 16
PAGE_SIZE = 128
WB = D_MODEL // 4
WA = (D_MODEL - WB) // 2
WC = D_MODEL - WA - WB
NA = 64
HA = WA // NA
LORA_W = 64
LORA_A = 64
LORA_G = 128
DVB = 128
DKB = DVB // 2
HB = WB // DVB
MASK_VALUE = -1e30
DVC = 128
DKC = 128
HC = WC // DVC
F_FLOOR = 1e-30
D_FF = ((8 * D_MODEL // 3 + 255) // 256) * 256
CONV_W = 3
PA = 3 * WA + LORA_W + LORA_A + LORA_G
PB = 2 * HB * 2 * DKB + HB * DVB
PC = 2 * HC * DKC + 2 * WC

LANES = 128
HEAD_PAIRS = HA // 2
HG_PIECES = 3
VMEM_LIMIT = 56 * 1024 * 1024


def _dot(a, b):
    return jnp.dot(a, b, preferred_element_type=F32)


def _dot_nt(a, b):
    return lax.dot_general(a, b, (((1,), (1,)), ((), ())), preferred_element_type=F32)


def _dot_tn(a, b):
    return lax.dot_general(a, b, (((0,), (0,)), ((), ())), preferred_element_type=F32)


def _bf16_pieces(x, n):
    pieces = []
    for _ in range(n - 1):
        p = x.astype(BF16)
        pieces.append(p)
        x = x - p.astype(F32)
    pieces.append(x.astype(BF16))
    return pieces


def _dot_exact_lhs(m16, x, n):
    ps = _bf16_pieces(x, n)
    acc = _dot(m16, ps[-1])
    for p in ps[-2::-1]:
        acc = acc + _dot(m16, p)
    return acc


def _dot_x3(a, b):
    ah, al = _bf16_pieces(a, 2)
    bh, bl = _bf16_pieces(b, 2)
    return _dot(ah, bh) + (_dot(al, bh) + _dot(ah, bl))


def _sigmoid(x):
    return 1.0 / (1.0 + jnp.exp(-x))


def _softplus(x):
    return jnp.maximum(x, 0.0) + jnp.log(1.0 + jnp.exp(-jnp.abs(x)))


def _mm_body(*refs, n_x, norm, eps, residual, sub):
    refs = list(refs)
    x_refs = [refs.pop(0) for _ in range(n_x)]
    g_ref = refs.pop(0) if norm else None
    w_ref = refs.pop(0)
    r_ref = refs.pop(0) if residual else None
    o_ref, xs_ref = refs

    @pl.when(pl.program_id(1) == 0)
    def _():
        tm = xs_ref.shape[0]
        col = 0
        for x_ref in x_refs:
            width = x_ref.shape[1]
            for s in range(0, tm, sub):
                x = x_ref[s:s + sub, :]
                if norm:
                    ms = jnp.mean(x * x, axis=-1, keepdims=True)
                    x = x * lax.rsqrt(ms + eps) * g_ref[...]
                xs_ref[s:s + sub, col:col + width] = x.astype(BF16)
            col += width

    acc = _dot(xs_ref[...], w_ref[...].astype(BF16))
    if residual:
        acc = acc + r_ref[...]
    o_ref[...] = acc


def _row_tile(m):
    for t in (1032, 688, 512, 344, 256, 128, 64, 32, 16, 8):
        if m % t == 0:
            return t
    raise ValueError(m)


def _matmul(x, w, layer, *, n_cols, col_off=0, tn=512, gain=None, eps=1e-6, residual=None):
    xs = x if isinstance(x, tuple) else (x,)
    m = xs[0].shape[0]
    k = sum(p.shape[1] for p in xs)
    tm = _row_tile(m)
    sub = 344 if tm % 344 == 0 else tm
    norm = gain is not None
    assert not norm or len(xs) == 1
    in_specs = [pl.BlockSpec((tm, p.shape[1]), lambda i, j: (i, 0)) for p in xs]
    args = list(xs)
    if norm:
        in_specs.append(pl.BlockSpec((None, 1, k), lambda i, j: (layer, 0, 0)))
        args.append(gain.reshape(gain.shape[0], 1, k))
    in_specs.append(pl.BlockSpec((None, k, tn), lambda i, j: (layer, 0, j + col_off)))
    args.append(w)
    if residual is not None:
        in_specs.append(pl.BlockSpec((tm, tn), lambda i, j: (i, j)))
        args.append(residual)
    return pl.pallas_call(
        functools.partial(_mm_body, n_x=len(xs), norm=norm, eps=eps, residual=residual is not None, sub=sub),
        grid=(m // tm, n_cols // tn),
        in_specs=in_specs,
        out_specs=pl.BlockSpec((tm, tn), lambda i, j: (i, j)),
        out_shape=jax.ShapeDtypeStruct((m, n_cols), F32),
        scratch_shapes=[pltpu.VMEM((tm, k), BF16)],
        compiler_params=pltpu.CompilerParams(
            dimension_semantics=("parallel", "arbitrary"), vmem_limit_bytes=VMEM_LIMIT),
        name="matmul",
    )(*args)


def _norm_body(x_ref, g_ref, o_ref, *, eps):
    x = x_ref[...]
    ms = jnp.mean(x * x, axis=-1, keepdims=True)
    o_ref[...] = x * lax.rsqrt(ms + eps) * g_ref[...]


def _rmsnorm(x3, gain, skip, eps=1e-6):
    if skip == 0:
        x3 = x3.reshape(1, -1, x3.shape[-1])
    b, t, k = x3.shape
    rows = t - skip
    tm = next(c for c in (256, 128, 64, 32, 16, 8) if rows % c == 0)
    return pl.pallas_call(
        functools.partial(_norm_body, eps=eps),
        grid=(b, rows // tm),
        in_specs=[pl.BlockSpec((pl.Element(1), pl.Element(tm), pl.Element(k)), lambda i, r: (i, pl.multiple_of(skip + r * tm, 8), 0)),
                  pl.BlockSpec((1, 1, k), lambda i, r: (0, 0, 0))],
        out_specs=pl.BlockSpec((1, tm, k), lambda i, r: (i, r, 0)),
        out_shape=jax.ShapeDtypeStruct((b, rows, k), F32),
        compiler_params=pltpu.CompilerParams(dimension_semantics=("parallel", "parallel")),
        name="final_norm",
    )(x3, gain.reshape(1, 1, k))


def _conv_body(ug_ref, uv_ref, bg_ref, bv_ref, wg_ref, wv_ref, cg_ref, cv_ref, o_ref):
    def conv(u_ref, buf_ref, w_ref, c_ref):
        u = u_ref[...]
        buf = buf_ref[...]
        w = w_ref[...]
        row = lax.broadcasted_iota(jnp.int32, (1, u.shape[1], 1), 1)
        b0 = buf[:, 0:1, :]
        b1 = buf[:, 1:2, :]
        u1 = jnp.where(row == 0, b1, pltpu.roll(u, 1, 1))
        u2 = jnp.where(row == 0, b0, jnp.where(row == 1, b1, pltpu.roll(u, 2, 1)))
        return c_ref[...] + u2 * w[0:1, :] + u1 * w[1:2, :] + u * w[2:3, :]

    gate = conv(ug_ref, bg_ref, wg_ref, cg_ref)
    val = conv(uv_ref, bv_ref, wv_ref, cv_ref)
    o_ref[...] = (gate * _sigmoid(gate) * val).astype(o_ref.dtype)


def _conv_gate(u3, buf, w_conv, b_conv, layer, *, tn=512):
    b, t, _ = u3.shape
    bb = b if t <= 64 else 1
    nj = D_FF // tn
    b_conv3 = b_conv.reshape(DEPTH, 1, 2 * D_FF)
    blk = lambda off: pl.BlockSpec((bb, t, tn), lambda i, j: (i, 0, j + off))
    bufs = lambda off: pl.BlockSpec((None, bb, CONV_W - 1, tn), lambda i, j: (layer, i, 0, j + off))
    ws = lambda off: pl.BlockSpec((None, CONV_W, tn), lambda i, j: (layer, 0, j + off))
    cs = lambda off: pl.BlockSpec((None, 1, tn), lambda i, j: (layer, 0, j + off))
    return pl.pallas_call(
        _conv_body,
        grid=(b // bb, nj),
        in_specs=[blk(0), blk(nj), bufs(0), bufs(nj), ws(0), ws(nj), cs(0), cs(nj)],
        out_specs=pl.BlockSpec((bb, t, tn), lambda i, j: (i, 0, j)),
        out_shape=jax.ShapeDtypeStruct((b, t, D_FF), BF16),
        compiler_params=pltpu.CompilerParams(
            dimension_semantics=("parallel", "parallel"), vmem_limit_bytes=VMEM_LIMIT),
        name="conv_gate",
    )(u3, u3, buf, buf, w_conv, w_conv, b_conv3, b_conv3)


def _up_conv_body(x_ref, g_ref, wg_ref, wv_ref, bg_ref, bv_ref, cwg_ref, cwv_ref, cbg_ref, cbv_ref,
                  act_ref, sg_ref, sv_ref, xs_ref, carry_ref, *, eps, sub, tiles_per_seq):
    i = pl.program_id(0)
    j = pl.program_id(1)
    tm = x_ref.shape[0]

    @pl.when((i == 0) & (j == 0))
    def _():
        carry_ref[...] = jnp.zeros_like(carry_ref)

    @pl.when(j == 0)
    def _():
        for s in range(0, tm, sub):
            x = x_ref[s:s + sub, :]
            ms = jnp.mean(x * x, axis=-1, keepdims=True)
            xs_ref[s:s + sub, :] = (x * lax.rsqrt(ms + eps) * g_ref[...]).astype(BF16)

    first = (i % tiles_per_seq) == 0
    row = lax.broadcasted_iota(jnp.int32, (tm, 1), 0)

    u_gate = _dot(xs_ref[...], wg_ref[...].astype(BF16))
    u_val = _dot(xs_ref[...], wv_ref[...].astype(BF16))

    def conv(u, buf_ref, cw_ref, cb_ref, s_ref, slot):
        before = jnp.where(first, buf_ref[0], carry_ref[j, slot])
        tail = u[tm - (CONV_W - 1):, :]
        carry_ref[j, slot] = tail
        s_ref[0] = tail
        cw = cw_ref[...]
        u1 = jnp.where(row == 0, before[1:2, :], pltpu.roll(u, 1, 0))
        u2 = jnp.where(row == 0, before[0:1, :], jnp.where(row == 1, before[1:2, :], pltpu.roll(u, 2, 0)))
        return cb_ref[...] + u2 * cw[0:1, :] + u1 * cw[1:2, :] + u * cw[2:3, :]

    gate = conv(u_gate, bg_ref, cwg_ref, cbg_ref, sg_ref, 0)
    val = conv(u_val, bv_ref, cwv_ref, cbv_ref, sv_ref, 1)
    act_ref[...] = (gate * _sigmoid(gate) * val).astype(act_ref.dtype)


def _up_conv_gate(h, gain, w_up, buf, w_conv, b_conv, layer, t, *, tn=512, eps=1e-6):
    m, k = h.shape
    b = m // t
    tm = _row_tile(t)
    tiles_per_seq = t // tm
    nj = D_FF // tn
    b_conv3 = b_conv.reshape(DEPTH, 1, 2 * D_FF)
    wsp = lambda off: pl.BlockSpec((None, k, tn), lambda i, j: (layer, 0, j + off))
    bufs = lambda off: pl.BlockSpec((None, 1, CONV_W - 1, tn), lambda i, j: (layer, i // tiles_per_seq, 0, j + off))
    cws = lambda off: pl.BlockSpec((None, CONV_W, tn), lambda i, j: (layer, 0, j + off))
    cbs = lambda off: pl.BlockSpec((None, 1, tn), lambda i, j: (layer, 0, j + off))
    tail_spec = pl.BlockSpec((1, CONV_W - 1, tn), lambda i, j: (i, 0, j))
    act, sg, sv = pl.pallas_call(
        functools.partial(_up_conv_body, eps=eps, sub=344 if tm % 344 == 0 else tm, tiles_per_seq=tiles_per_seq),
        grid=(m // tm, nj),
        in_specs=[
            pl.BlockSpec((tm, k), lambda i, j: (i, 0)),
            pl.BlockSpec((None, 1, k), lambda i, j: (layer, 0, 0)),
            wsp(0), wsp(nj), bufs(0), bufs(nj), cws(0), cws(nj), cbs(0), cbs(nj),
        ],
        out_specs=[pl.BlockSpec((tm, tn), lambda i, j: (i, j)), tail_spec, tail_spec],
        out_shape=[
            jax.ShapeDtypeStruct((m, D_FF), BF16),
            jax.ShapeDtypeStruct((m // tm, CONV_W - 1, D_FF), F32),
            jax.ShapeDtypeStruct((m // tm, CONV_W - 1, D_FF), F32),
        ],
        scratch_shapes=[pltpu.VMEM((tm, k), BF16), pltpu.VMEM((nj, 2, CONV_W - 1, tn), F32)],
        compiler_params=pltpu.CompilerParams(
            dimension_semantics=("arbitrary", "arbitrary"), vmem_limit_bytes=VMEM_LIMIT),
        name="up_conv_gate",
    )(h, gain.reshape(DEPTH, 1, k), w_up, w_up, buf, buf, w_conv, w_conv, b_conv3, b_conv3)
    seq_end = slice(tiles_per_seq - 1, None, tiles_per_seq)
    return act, jnp.concatenate([sg[seq_end], sv[seq_end]], axis=-1)


def _seg_sum64(x, bd16):
    n = x.shape[0]
    st = jnp.concatenate(_bf16_pieces(x, 2), axis=0)
    outs = []
    for j in range(x.shape[1] // LANES):
        y = _dot(st[:, j * LANES:(j + 1) * LANES], bd16)
        outs.append(y[:n] + y[n:])
    return jnp.concatenate(outs, axis=1)


def _rwkv_body(x_ref, shift_ref, s0_ref, mu_ref, w0_ref, wa2_ref, a0_ref, g2_ref, kk_ref, ka_ref, rk_ref,
               lnw_ref, lnb_ref, y_ref, sout_ref, prev_scr, s_scr, *, nb, chunk, n_sq):
    c = pl.program_id(1)
    n = nb * chunk

    @pl.when(c == 0)
    def _():
        prev_scr[...] = shift_ref[...]
        zero = jnp.zeros((NA, NA), F32)
        for i in range(nb):
            for j in range(HEAD_PAIRS):
                top = jnp.concatenate([s0_ref[i, 2 * j], zero], axis=1)
                bot = jnp.concatenate([zero, s0_ref[i, 2 * j + 1]], axis=1)
                s_scr[i, j] = jnp.concatenate([top, bot], axis=0)

    x = x_ref[...].reshape(n, PA)
    row = lax.broadcasted_iota(jnp.int32, (n, 1), 0)
    prev = pltpu.roll(x, 1, 0)
    for i in range(nb):
        prev = jnp.where(row == i * chunk, prev_scr[i], prev)
        prev_scr[i] = x[(i + 1) * chunk - 1:(i + 1) * chunk, :]
    xm = x + (prev - x) * mu_ref[...]
    r = xm[:, :WA]
    k = xm[:, WA:2 * WA]
    v = xm[:, 2 * WA:3 * WA]
    wa_in = xm[:, 3 * WA:3 * WA + LANES]
    gd = xm[:, 3 * WA + LANES:]

    lane = lax.broadcasted_iota(jnp.int32, (1, LANES), 1)
    lo = lane < NA
    wa2 = wa2_ref[...]
    z = w0_ref[...] + _dot_x3(jnp.where(lo, jnp.tanh(wa_in), 0.0), wa2)
    w = -_softplus(-z) - 0.5
    logd = -jnp.exp(w)
    a = _sigmoid(a0_ref[...] + _dot(jnp.where(lo, 0.0, wa_in).astype(BF16), wa2.astype(BF16)))
    g = _dot(_sigmoid(gd).astype(BF16), g2_ref[...].astype(BF16))

    ri = lax.broadcasted_iota(jnp.int32, (LANES, LANES), 0)
    ci = lax.broadcasted_iota(jnp.int32, (LANES, LANES), 1)
    bd_mask = (ri < NA) == (ci < NA)
    bd16 = jnp.where(bd_mask, 1.0, 0.0).astype(BF16)
    eye = ri == ci

    kkr = k * kk_ref[...]
    kk = kkr / jnp.maximum(jnp.sqrt(_seg_sum64(kkr * kkr, bd16)), 1e-12)
    k2 = k * (1.0 + (a - 1.0) * ka_ref[...])
    b = kk * a

    tn_ = lax.broadcasted_iota(jnp.int32, (n, n), 0)
    sn_ = lax.broadcasted_iota(jnp.int32, (n, n), 1)
    same = None
    for i in range(nb):
        blk = ((tn_ >= i * chunk) & (tn_ < (i + 1) * chunk) & (sn_ >= i * chunk) & (sn_ < (i + 1) * chunk))
        same = blk if same is None else (same | blk)
    tri_incl = jnp.where(same & (sn_ <= tn_), 1.0, 0.0).astype(BF16)
    tri_rest = jnp.where(same & (sn_ > tn_), 1.0, 0.0).astype(BF16)
    cl = _dot_exact_lhs(tri_incl, logd, 3)
    rl = _dot_exact_lhs(tri_rest, logd, 3)
    e_neg = jnp.exp(-cl)
    e_end = jnp.exp(rl)
    alpha = kk * jnp.exp(cl - logd)
    rho = r * jnp.exp(cl)
    beta = b * e_neg
    kappa = k2 * e_neg
    beta_e = (b * e_end).astype(BF16)
    kappa_e = (k2 * e_end).astype(BF16)
    v16 = v.astype(BF16)

    ti = lax.broadcasted_iota(jnp.int32, (chunk, 2 * chunk), 0)
    cz = lax.broadcasted_iota(jnp.int32, (chunk, 2 * chunk), 1)
    first = cz < chunk
    si = jnp.where(first, cz, cz - chunk)
    incl = si <= ti
    strict = si < ti
    eye_z = jnp.where(ti == si, 1.0, 0.0).astype(F32)
    lane2 = lax.broadcasted_iota(jnp.int32, (1, 2 * LANES), 1)
    lo2 = (lane2 < NA) | ((lane2 >= LANES) & (lane2 < LANES + NA))

    def by_head(x):
        m = lo if x.shape[1] == LANES else lo2
        return jnp.concatenate([jnp.where(m, x, 0.0), jnp.where(m, 0.0, x)], axis=0).astype(BF16)

    def block_diag(z):
        return jnp.concatenate([jnp.where(first, z, 0.0), jnp.where(first, 0.0, z)], axis=0).astype(BF16)

    rows = lambda i: slice(i * chunk, (i + 1) * chunk)
    lanes = lambda j: slice(j * LANES, (j + 1) * LANES)
    pairs = [(i, j) for i in range(nb) for j in range(HEAD_PAIRS)]
    cut = lambda x: {p: x[rows(p[0]), lanes(p[1])] for p in pairs}

    a_p, r_p, v_p = cut(alpha), cut(rho), cut(v16)
    ar = {p: jnp.concatenate([a_p[p], r_p[p]], axis=0).astype(BF16) for p in pairs}
    b_rows = {p: by_head(x) for p, x in cut(beta).items()}
    k_rows = {p: by_head(x) for p, x in cut(kappa).items()}
    v_rows = {p: by_head(x) for p, x in cut(v).items()}
    mb = {p: _dot_nt(ar[p], b_rows[p]) for p in pairs}
    mk = {p: _dot_nt(ar[p], k_rows[p]) for p in pairs}
    mab = {p: jnp.where(strict, mb[p][:chunk], 0.0) for p in pairs}
    mrb = {p: jnp.where(incl, mb[p][chunk:], 0.0).astype(BF16) for p in pairs}
    mak = {p: jnp.where(strict, mk[p][:chunk], 0.0).astype(BF16) for p in pairs}
    mrk = {p: jnp.where(incl, mk[p][chunk:], 0.0).astype(BF16) for p in pairs}
    xh = {p: _dot(mak[p], v_rows[p]) for p in pairs}
    ov = {p: _dot(mrk[p], v_rows[p]) for p in pairs}
    pw = {p: -mab[p] for p in pairs}
    tinv = {p: eye_z - mab[p] for p in pairs}
    pw_bd = {p: block_diag(pw[p]) for p in pairs}
    for _ in range(n_sq):
        pw = {p: _dot(pw[p].astype(BF16), pw_bd[p]) for p in pairs}
        pw_bd = {p: block_diag(pw[p]) for p in pairs}
        tinv = {p: tinv[p] + _dot(tinv[p].astype(BF16), pw_bd[p]) for p in pairs}
    wu = {p: _dot(tinv[p].astype(BF16), by_head(jnp.concatenate([a_p[p], xh[p]], axis=1))) for p in pairs}
    y2 = {p: _dot(mrb[p], by_head(wu[p])) for p in pairs}
    w2 = {p: wu[p][:, :LANES] for p in pairs}
    u0 = {p: wu[p][:, LANES:] for p in pairs}
    rw = {p: r_p[p] - y2[p][:, :LANES] for p in pairs}
    o0 = {p: ov[p] - y2[p][:, LANES:] for p in pairs}

    s_old = {p: s_scr[p[0], p[1]] for p in pairs}
    s16 = {p: s_old[p].astype(BF16) for p in pairs}
    out = {p: _dot_nt(rw[p].astype(BF16), s16[p]) + o0[p] for p in pairs}
    w16 = {p: w2[p].astype(BF16) for p in pairs}
    be_p = {p: beta_e[rows(p[0]), lanes(p[1])] for p in pairs}
    ke_p = {p: kappa_e[rows(p[0]), lanes(p[1])] for p in pairs}
    wtb = {p: _dot_tn(w16[p], be_p[p]) for p in pairs}
    gm = {p: _dot_tn(jnp.concatenate([v_p[p], (-u0[p]).astype(BF16)], axis=0),
                     jnp.concatenate([ke_p[p], be_p[p]], axis=0)) for p in pairs}
    for p in pairs:
        i, j = p
        p_end = jnp.exp(cl[(i + 1) * chunk - 1:(i + 1) * chunk, lanes(j)])
        phi = jnp.where(bd_mask, jnp.where(eye, p_end, 0.0) - wtb[p], 0.0)
        s_scr[i, j] = _dot(s16[p], phi.astype(BF16)) + jnp.where(bd_mask, gm[p], 0.0)
    o = jnp.concatenate(
        [jnp.concatenate([out[(i, j)] for j in range(HEAD_PAIRS)], axis=1) for i in range(nb)], axis=0)

    mean = _seg_sum64(o, bd16) * (1.0 / NA)
    oc = o - mean
    var = _seg_sum64(oc * oc, bd16) * (1.0 / NA)
    o = oc * lax.rsqrt(var + 64e-5) * lnw_ref[...] + lnb_ref[...]
    o = o + _seg_sum64(r * k2 * rk_ref[...], bd16) * v
    y_ref[...] = (o * g).reshape(nb, chunk, WA).astype(y_ref.dtype)

    @pl.when(c == pl.num_programs(1) - 1)
    def _():
        for i in range(nb):
            for j in range(HEAD_PAIRS):
                s_pair = s_scr[i, j]
                sout_ref[i, 2 * j] = s_pair[:NA, :NA]
                sout_ref[i, 2 * j + 1] = s_pair[NA:, NA:]


def _n_squarings(chunk):
    n, reach = 0, 1
    while reach < chunk - 1:
        n += 1
        reach = 2 * reach + 1
    return n


def _rwkv(pa3, shift0, s0, params, layer, chunk, nb):
    b, t, _ = pa3.shape
    nc = t // chunk
    vec = lambda n: pl.BlockSpec((None, 1, n), lambda i, c: (layer, 0, 0))
    mat = lambda r, n: pl.BlockSpec((None, r, n), lambda i, c: (layer, 0, 0))
    y, s_out = pl.pallas_call(
        functools.partial(_rwkv_body, nb=nb, chunk=chunk, n_sq=_n_squarings(chunk)),
        grid=(b // nb, nc),
        in_specs=[
            pl.BlockSpec((nb, chunk, PA), lambda i, c: (i, c, 0)),
            pl.BlockSpec((nb, 1, PA), lambda i, c: (i, 0, 0)),
            pl.BlockSpec((nb, HA, NA, NA), lambda i, c: (i, 0, 0, 0)),
            vec(PA), vec(WA), mat(LANES, WA), vec(WA), mat(LORA_G, WA), vec(WA), vec(WA), vec(WA), vec(WA), vec(WA),
        ],
        out_specs=[
            pl.BlockSpec((nb, chunk, WA), lambda i, c: (i, c, 0)),
            pl.BlockSpec((nb, HA, NA, NA), lambda i, c: (i, 0, 0, 0)),
        ],
        out_shape=[
            jax.ShapeDtypeStruct((b, t, WA), BF16),
            jax.ShapeDtypeStruct((b, HA, NA, NA), F32),
        ],
        scratch_shapes=[pltpu.VMEM((nb, 1, PA), F32), pltpu.VMEM((nb, HEAD_PAIRS, LANES, LANES), F32)],
        compiler_params=pltpu.CompilerParams(
            dimension_semantics=("parallel", "arbitrary"), vmem_limit_bytes=VMEM_LIMIT),
        name="rwkv7",
    )(pa3, shift0.reshape(b, 1, PA), s0, *params)
    return y, s_out


def _hgrn_body(*refs, nb, chunk):
    pieces = refs[:4 * HG_PIECES]
    s0_ref, lb_ref, nw_ref, y_ref, sout_ref, s_scr = refs[4 * HG_PIECES:]
    group = lambda n, i: jnp.concatenate([r[i] for r in pieces[n * HG_PIECES:(n + 1) * HG_PIECES]], axis=1)
    c = pl.program_id(1)

    @pl.when(c == 0)
    def _():
        s_scr[...] = s0_ref[...]

    lb = lb_ref[...]
    ti = lax.broadcasted_iota(jnp.int32, (chunk, chunk), 0)
    si = lax.broadcasted_iota(jnp.int32, (chunk, chunk), 1)
    incl = si <= ti
    tri = jnp.where(incl, 1.0, 0.0).astype(BF16)
    mid = (chunk - 1) // 2
    sl = lambda h: slice(h * LANES, (h + 1) * LANES)

    qa, ka, qe, kl, iv, last = {}, {}, {}, {}, {}, {}
    for i in range(nb):
        q = group(0, i)
        fl = group(1, i)
        f = lb + (1.0 - lb) * _sigmoid(fl)
        log_f = jnp.log(jnp.maximum(f, F_FLOOR))
        key = (1.0 - lb) * _sigmoid(-fl)
        cum = _dot_exact_lhs(tri, log_f, 3)
        anchor = cum[mid:mid + 1, :]
        last[i] = cum[chunk - 1:chunk, :]
        qa[i] = (q * jnp.exp(cum - anchor)).astype(BF16)
        ka[i] = (key * jnp.exp(anchor - cum)).astype(BF16)
        qe[i] = (q * jnp.exp(cum)).astype(BF16)
        kl[i] = (key * jnp.exp(last[i] - cum)).astype(BF16)
        iv[i] = group(2, i).astype(BF16)

    chains = [(i, h) for i in range(nb) for h in range(HC)]
    s_old = {ch: s_scr[ch[0], ch[1]] for ch in chains}
    att = {(i, h): jnp.where(incl, _dot_nt(qa[i][:, sl(h)], ka[i][:, sl(h)]), 0.0).astype(BF16) for (i, h) in chains}
    inter = {(i, h): _dot(qe[i][:, sl(h)], s_old[(i, h)].astype(BF16)) for (i, h) in chains}
    upd = {(i, h): _dot_tn(kl[i][:, sl(h)], iv[i][:, sl(h)]) for (i, h) in chains}
    intra = {(i, h): _dot(att[(i, h)], iv[i][:, sl(h)]) for (i, h) in chains}
    for i in range(nb):
        outs = []
        for h in range(HC):
            o = intra[(i, h)] + inter[(i, h)]
            decay = jnp.exp(jnp.broadcast_to(last[i][:, sl(h)], (DKC, LANES)).T)
            s_scr[i, h] = decay * s_old[(i, h)] + upd[(i, h)]
            ms = jnp.mean(o * o, axis=-1, keepdims=True)
            outs.append(o * lax.rsqrt(ms + 1e-6) * nw_ref[...])
        g = group(3, i)
        y_ref[i] = (jnp.concatenate(outs, axis=1) * (g * _sigmoid(g))).astype(y_ref.dtype)

    @pl.when(c == pl.num_programs(1) - 1)
    def _():
        sout_ref[...] = s_scr[...]


def _hgrn(p3, s0, lb, norm_w, layer, chunk, nb):
    b, t, _ = p3.shape
    nc = t // chunk
    width = WC // HG_PIECES
    c0 = (PA + PB) // width
    col = lambda n: pl.BlockSpec((nb, chunk, width), lambda i, c: (i, c, c0 + n))
    y, s_out = pl.pallas_call(
        functools.partial(_hgrn_body, nb=nb, chunk=chunk),
        grid=(b // nb, nc),
        in_specs=[
            *[col(n) for n in range(4 * HG_PIECES)],
            pl.BlockSpec((nb, HC, DKC, DVC), lambda i, c: (i, 0, 0, 0)),
            pl.BlockSpec((None, 1, WC), lambda i, c: (layer, 0, 0)),
            pl.BlockSpec((None, 1, DVC), lambda i, c: (layer, 0, 0)),
        ],
        out_specs=[
            pl.BlockSpec((nb, chunk, WC), lambda i, c: (i, c, 0)),
            pl.BlockSpec((nb, HC, DKC, DVC), lambda i, c: (i, 0, 0, 0)),
        ],
        out_shape=[
            jax.ShapeDtypeStruct((b, t, WC), BF16),
            jax.ShapeDtypeStruct((b, HC, DKC, DVC), F32),
        ],
        scratch_shapes=[pltpu.VMEM((nb, HC, DKC, DVC), F32)],
        compiler_params=pltpu.CompilerParams(
            dimension_semantics=("parallel", "arbitrary"), vmem_limit_bytes=VMEM_LIMIT),
        name="hgrn2",
    )(*([p3] * (4 * HG_PIECES)), s0, lb.reshape(DEPTH, 1, WC), norm_w.reshape(DEPTH, 1, DVC))
    return y, s_out


def _split_maps(q, scale):
    lane = lax.broadcasted_iota(jnp.int32, (1, LANES), 1)
    q = q * scale
    return jnp.concatenate([jnp.where(lane < DKB, q, 0.0), jnp.where(lane < DKB, 0.0, q)], axis=0).astype(BF16)


def _subln(o, w, scale):
    ms = jnp.mean(o * o, axis=-1, keepdims=True)
    return o * lax.rsqrt(ms + 1e-5) * w * scale


def _attn_prompt_body(lam_ref, q_ref, k_ref, v_ref, w_ref, *refs, tq, nq, out_scale, n_carried):
    o_ref, ko_ref, vo_ref = refs[n_carried:]
    h = pl.program_id(1)
    qi = pl.program_id(2)
    t = k_ref.shape[1]
    lam = lam_ref[0]

    @pl.when(qi == 0)
    def _():
        for hh in range(HB):
            @pl.when(h == hh)
            def _(hh=hh):
                ko_ref[0, pl.ds(hh, t, stride=HB), :] = k_ref[0]
                vo_ref[0, pl.ds(hh, t, stride=HB), :] = v_ref[0]

    qq = _split_maps(q_ref[0], DKB ** -0.5)
    for n in range(nq):
        @pl.when(qi == n)
        def _(n=n):
            ext = (n + 1) * tq
            s = _dot_nt(qq, k_ref[0, :ext, :].astype(BF16))
            qpos = n * tq + lax.broadcasted_iota(jnp.int32, (tq, 1), 0)
            qpos = jnp.concatenate([qpos, qpos], axis=0)
            kpos = lax.broadcasted_iota(jnp.int32, (1, ext), 1)
            s = jnp.where(kpos <= qpos, s, MASK_VALUE)
            e = jnp.exp(s - jnp.max(s, axis=-1, keepdims=True))
            p = e * (1.0 / jnp.sum(e, axis=-1, keepdims=True))
            att = p[:tq] - lam * p[tq:]
            o = _dot(att.astype(BF16), v_ref[0, :ext, :].astype(BF16))
            o_ref[0] = _subln(o, w_ref[...], out_scale).astype(o_ref.dtype)


def _attn_prompt(p3, lam, subln_w, layer, out_scale, kv_all=None):
    b, t, _ = p3.shape
    tq = 344 if t % 344 == 0 else t
    c0 = PA // LANES
    kv_spec = pl.BlockSpec((None, 1, t * HB, LANES), lambda i, h, q: (layer, i, 0, 0))
    carried = () if kv_all is None else tuple(kv_all)
    n_in = 5
    yb, ko, vo = pl.pallas_call(
        functools.partial(_attn_prompt_body, tq=tq, nq=t // tq, out_scale=out_scale, n_carried=len(carried)),
        grid=(b, HB, t // tq),
        in_specs=[
            pl.BlockSpec(memory_space=pltpu.SMEM),
            pl.BlockSpec((1, tq, LANES), lambda i, h, q: (i, q, c0 + h)),
            pl.BlockSpec((1, t, LANES), lambda i, h, q: (i, 0, c0 + HB + h)),
            pl.BlockSpec((1, t, LANES), lambda i, h, q: (i, 0, c0 + 2 * HB + h)),
            pl.BlockSpec((None, 1, DVB), lambda i, h, q: (layer, 0, 0)),
        ] + [pl.BlockSpec(memory_space=pl.ANY)] * len(carried),
        out_specs=[pl.BlockSpec((1, tq, LANES), lambda i, h, q: (i, q, h)), kv_spec, kv_spec],
        out_shape=[
            jax.ShapeDtypeStruct((b, t, WB), BF16),
            jax.ShapeDtypeStruct((DEPTH, b, t * HB, 2 * DKB), F32),
            jax.ShapeDtypeStruct((DEPTH, b, t * HB, DVB), F32),
        ],
        input_output_aliases={n_in + n: 1 + n for n in range(len(carried))},
        compiler_params=pltpu.CompilerParams(
            dimension_semantics=("parallel", "arbitrary", "arbitrary"), vmem_limit_bytes=VMEM_LIMIT),
        name="diff_attn_prompt",
    )(lam, p3, p3, p3, subln_w.reshape(DEPTH, 1, DVB), *carried)
    return yb, (ko, vo)


def _attn_sample_body(pt_ref, lam_ref, *refs, n_pg, n_grp, t, out_scale):
    del pt_ref
    q_ref, kn_ref, vn_ref, w_ref = refs[:4]
    k_refs = refs[4:4 + n_pg]
    v_refs = refs[4 + n_pg:4 + 2 * n_pg]
    o_ref, qq_scr, m_scr, l_scr, acc_scr = refs[4 + 2 * n_pg:]
    g = pl.program_id(1)
    hr = 2 * t
    page_rows = PAGE_SIZE * HB

    @pl.when(g == 0)
    def _():
        q = q_ref[0]
        qpos = lax.broadcasted_iota(jnp.int32, (t, 1), 0)
        qpos = jnp.concatenate([qpos, qpos], axis=0)
        kpos = lax.broadcasted_iota(jnp.int32, (1, t), 1)
        for h in range(HB):
            sl = slice(h * LANES, (h + 1) * LANES)
            rs = slice(h * hr, (h + 1) * hr)
            qq = _split_maps(q[:, sl], DKB ** -0.5)
            qq_scr[rs, :] = qq
            s = _dot_nt(qq, kn_ref[0][:, sl].astype(BF16))
            s = jnp.where(kpos <= qpos, s, MASK_VALUE)
            m = jnp.max(s, axis=-1, keepdims=True)
            e = jnp.exp(s - m)
            m_scr[0, rs, :] = m
            l_scr[0, rs, :] = jnp.sum(e, axis=-1, keepdims=True)
            acc_scr[0, rs, :] = _dot(e.astype(BF16), vn_ref[0][:, sl].astype(BF16))
        for grp in range(1, n_grp):
            m_scr[grp] = jnp.full(m_scr.shape[1:], MASK_VALUE, F32)
            l_scr[grp] = jnp.zeros(l_scr.shape[1:], F32)
            acc_scr[grp] = jnp.zeros(acc_scr.shape[1:], F32)

    qq = qq_scr[...]
    rid = lax.broadcasted_iota(jnp.int32, (HB * hr, 1), 0)
    row_head = sum((rid >= h * hr).astype(jnp.int32) for h in range(1, HB))
    col_head = lax.broadcasted_iota(jnp.int32, (1, page_rows), 1) & (HB - 1)
    own = row_head == col_head
    per = n_pg // n_grp
    groups = range(n_grp)
    pages = lambda grp: range(grp * per, (grp + 1) * per)
    s = {grp: jnp.concatenate([jnp.where(own, _dot_nt(qq, k_refs[n][...].astype(BF16)), MASK_VALUE)
                               for n in pages(grp)], axis=1) for grp in groups}
    m_old = {grp: m_scr[grp] for grp in groups}
    m_new = {grp: jnp.maximum(m_old[grp], jnp.max(s[grp], axis=-1, keepdims=True)) for grp in groups}
    e = {grp: jnp.exp(s[grp] - m_new[grp]).astype(BF16) for grp in groups}
    for grp in groups:
        corr = jnp.exp(m_old[grp] - m_new[grp])
        pv = None
        for idx, n in enumerate(pages(grp)):
            part = _dot(e[grp][:, idx * page_rows:(idx + 1) * page_rows], v_refs[n][...].astype(BF16))
            pv = part if pv is None else pv + part
        l_scr[grp] = corr * l_scr[grp] + jnp.sum(e[grp].astype(F32), axis=-1, keepdims=True)
        acc_scr[grp] = corr * acc_scr[grp] + pv
        m_scr[grp] = m_new[grp]

    @pl.when(g == pl.num_programs(1) - 1)
    def _():
        lam = lam_ref[0]
        m_all = m_scr[0]
        for grp in range(1, n_grp):
            m_all = jnp.maximum(m_all, m_scr[grp])
        l_all = jnp.zeros_like(m_all)
        acc_all = jnp.zeros(acc_scr.shape[1:], F32)
        for grp in groups:
            w_grp = jnp.exp(m_scr[grp] - m_all)
            l_all = l_all + w_grp * l_scr[grp]
            acc_all = acc_all + w_grp * acc_scr[grp]
        o = acc_all * (1.0 / l_all)
        outs = []
        for h in range(HB):
            oh = o[h * hr:h * hr + t] - lam * o[h * hr + t:(h + 1) * hr]
            outs.append(_subln(oh, w_ref[...], out_scale))
        o_ref[0] = jnp.concatenate(outs, axis=1).astype(o_ref.dtype)


def _attn_sample(pb3, cache_k, cache_v, page_table, lam, subln_w, layer, out_scale):
    b, t, _ = pb3.shape
    n_pages = page_table.shape[1]
    n_pg = next(n for n in (16, 8, 4, 2, 1) if n_pages % n == 0)
    n_grp = next(n for n in (4, 2, 1) if n_pg % n == 0)
    depth, n_pool = cache_k.shape[:2]
    assert HB & (HB - 1) == 0 and 2 * DKB == DVB == LANES
    ck = cache_k.reshape(depth, n_pool, PAGE_SIZE * HB, 2 * DKB)
    cv = cache_v.reshape(depth, n_pool, PAGE_SIZE * HB, DVB)

    def page_spec(n):
        return pl.BlockSpec((None, None, PAGE_SIZE * HB, LANES),
                            lambda i, g, pt: (layer, pt[i, g * n_pg + n], 0, 0))

    rows = HB * 2 * t
    c0 = PA // WB
    grid_spec = pltpu.PrefetchScalarGridSpec(
        num_scalar_prefetch=1,
        grid=(b, n_pages // n_pg),
        in_specs=[
            pl.BlockSpec(memory_space=pltpu.SMEM),
            pl.BlockSpec((1, t, WB), lambda i, g, pt: (i, 0, c0)),
            pl.BlockSpec((1, t, WB), lambda i, g, pt: (i, 0, c0 + 1)),
            pl.BlockSpec((1, t, WB), lambda i, g, pt: (i, 0, c0 + 2)),
            pl.BlockSpec((None, 1, DVB), lambda i, g, pt: (layer, 0, 0)),
        ] + [page_spec(n) for n in range(n_pg)] * 2,
        out_specs=pl.BlockSpec((1, t, WB), lambda i, g, pt: (i, 0, 0)),
        scratch_shapes=[
            pltpu.VMEM((rows, LANES), BF16), pltpu.VMEM((n_grp, rows, 1), F32), pltpu.VMEM((n_grp, rows, 1), F32),
            pltpu.VMEM((n_grp, rows, DVB), F32)],
    )
    return pl.pallas_call(
        functools.partial(_attn_sample_body, n_pg=n_pg, n_grp=n_grp, t=t, out_scale=out_scale),
        grid_spec=grid_spec,
        out_shape=jax.ShapeDtypeStruct((b, t, WB), BF16),
        compiler_params=pltpu.CompilerParams(
            dimension_semantics=("parallel", "arbitrary"), vmem_limit_bytes=VMEM_LIMIT),
        name="diff_attn_sample",
    )(page_table, lam, pb3, pb3, pb3, subln_w.reshape(DEPTH, 1, DVB), *([ck] * n_pg), *([cv] * n_pg))


def _trunk(h3, paged, rwkv0, shift0, hgrn0, conv0, wt, chunk, nb, skip):
    b, t, _ = h3.shape
    m = b * t
    h = h3.reshape(m, D_MODEL)
    lbs = jax.nn.softmax(wt['hg_lb'].astype(F32), axis=0)
    lbs = jnp.cumsum(lbs, axis=0) - lbs[0]
    wa2 = jnp.concatenate([wt['rw_w2'], wt['rw_a2']], axis=1)
    row = lambda name, n: wt[name].reshape(DEPTH, 1, n)
    rw_params = (row('rw_mu', PA), row('rw_w0', WA), wa2, row('rw_a0', WA), wt['rw_g2'], row('rw_kk', WA),
                 row('rw_ka', WA), row('rw_rk', WA), row('rw_lnw', WA), row('rw_lnb', WA))
    ks, vs, sas, shs, scs, cvs = [], [], [], [], [], []
    kv_all = None
    for l in range(DEPTH):
        p3 = _matmul(h, wt['w_in'], l, n_cols=PA + PB + PC, tn=1024, gain=wt['norm1']).reshape(b, t, PA + PB + PC)
        ya, sa = _rwkv(p3, shift0[l], rwkv0[l], rw_params, l, chunk, nb)
        lam_init = 0.8 - 0.6 * math.exp(-0.3 * l)
        lam = (jnp.exp(jnp.sum(wt['da_lq1'][l].astype(F32) * wt['da_lk1'][l].astype(F32)))
               - jnp.exp(jnp.sum(wt['da_lq2'][l].astype(F32) * wt['da_lk2'][l].astype(F32))) + lam_init)
        lam = lam.reshape(1).astype(F32)
        if paged is None:
            yb, kv_all = _attn_prompt(p3, lam, wt['da_subln'], l, 1.0 - lam_init, kv_all)
        else:
            yb = _attn_sample(p3, paged[0], paged[1], paged[2], lam, wt['da_subln'], l, 1.0 - lam_init)
            ks.append(p3[:, :, PA + WB:PA + 2 * WB].reshape(b, t, HB, 2 * DKB))
            vs.append(p3[:, :, PA + 2 * WB:PA + 3 * WB].reshape(b, t, HB, DVB))
        yc, sc = _hgrn(p3, hgrn0[l], lbs, wt['hg_norm'], l, chunk, next(n for n in (4, 2, 1) if b % n == 0))
        mix = (ya.reshape(m, WA), yb.reshape(m, WB), yc.reshape(m, WC))
        h = _matmul(mix, wt['w_out'], l, n_cols=D_MODEL, tn=1024, residual=h)
        if t % 344 == 0:
            act, cb = _up_conv_gate(h, wt['norm2'], wt['ffn_up'], conv0, wt['ffn_conv'], wt['ffn_conv_b'], l, t)
        else:
            u3 = _matmul(h, wt['ffn_up'], l, n_cols=2 * D_FF, gain=wt['norm2']).reshape(b, t, 2 * D_FF)
            act = _conv_gate(u3, conv0, wt['ffn_conv'], wt['ffn_conv_b'], l).reshape(m, D_FF)
            cb = u3[:, t - (CONV_W - 1):, :]
        h = _matmul(act, wt['ffn_down'], l, n_cols=D_MODEL, tn=512, residual=h)
        sas.append(sa)
        shs.append(p3[:, t - 1, :PA])
        scs.append(sc)
        cvs.append(cb)
    y = _rmsnorm(h.reshape(b, t, D_MODEL), wt['final_norm'], skip).reshape(b, t - skip, D_MODEL)
    if paged is None:
        k_out = kv_all[0].reshape(DEPTH, b, t, HB, 2 * DKB)
        v_out = kv_all[1].reshape(DEPTH, b, t, HB, DVB)
    else:
        k_out, v_out = jnp.stack(ks), jnp.stack(vs)
    return (y, k_out, v_out, jnp.stack(sas), jnp.stack(shs), jnp.stack(scs), jnp.stack(cvs))


def kernel(x_prompt, x_sample, cache_k, cache_v, state_rwkv, state_shift, state_hgrn, state_conv, page_table, meta, norm1, w_in, rw_mu, rw_w0, rw_w2, rw_a0, rw_a2, rw_g2, rw_kk, rw_ka, rw_rk, rw_lnw, rw_lnb, da_lq1, da_lk1, da_lq2, da_lk2, da_subln, hg_lb, hg_norm, w_out, norm2, ffn_up, ffn_conv, ffn_conv_b, ffn_down, final_norm):
    wt = {'norm1': norm1, 'w_in': w_in, 'rw_mu': rw_mu, 'rw_w0': rw_w0, 'rw_w2': rw_w2, 'rw_a0': rw_a0,
          'rw_a2': rw_a2, 'rw_g2': rw_g2, 'rw_kk': rw_kk, 'rw_ka': rw_ka, 'rw_rk': rw_rk, 'rw_lnw': rw_lnw,
          'rw_lnb': rw_lnb, 'da_lq1': da_lq1, 'da_lk1': da_lk1, 'da_lq2': da_lq2, 'da_lk2': da_lk2,
          'da_subln': da_subln, 'hg_lb': hg_lb, 'hg_norm': hg_norm, 'w_out': w_out, 'norm2': norm2,
          'ffn_up': ffn_up, 'ffn_conv': ffn_conv, 'ffn_conv_b': ffn_conv_b, 'ffn_down': ffn_down,
          'final_norm': final_norm}
    for name in ('w_in', 'w_out', 'ffn_down'):
        wt[name] = wt[name].astype(BF16)
    bp = x_prompt.shape[0]
    dt = x_prompt.dtype
    hp = jnp.concatenate([jnp.broadcast_to(meta[None].astype(dt), (bp, N_META, D_MODEL)), x_prompt], axis=1)
    tp = hp.shape[1]
    chunk_p = next(c for c in (48, 24, 16, 8) if tp % c == 0)
    y_prompt, p_k, p_v, p_rwkv, p_shift, p_hgrn, p_conv = _trunk(
        hp, None,
        jnp.zeros((DEPTH, bp, HA, NA, NA), dt), jnp.zeros((DEPTH, bp, PA), dt),
        jnp.zeros((DEPTH, bp, HC, DKC, DVC), dt), jnp.zeros((DEPTH, bp, CONV_W - 1, 2 * D_FF), dt),
        wt, chunk_p, next(n for n in (4, 2, 1) if bp % n == 0), N_META)
    bs, ts = x_sample.shape[:2]
    y_sample, s_k, s_v, s_rwkv, s_shift, s_hgrn, s_conv = _trunk(
        x_sample, (cache_k, cache_v, page_table), state_rwkv, state_shift, state_hgrn, state_conv, wt, ts,
        4 if bs % 4 == 0 else 1, 0)
    return (y_prompt, y_sample, p_k, p_v, p_rwkv, p_shift, p_hgrn, p_conv,
            s_k, s_v, s_rwkv, s_shift, s_hgrn, s_conv)
```
